```python
import math
import jax, jax.numpy as jnp
from jax import lax
import numpy as np

D_MODEL = 1024
BATCH = 16
SEQ = 2048
DEPTH = 2

MLA_HEADS = 8
MLA_NOPE = 64
MLA_ROPE = 32
MLA_V = 64
MLA_Q_LORA = 384
MLA_KV_LORA = 256
ROPE_THETA = 10000.0
DIFF_HEADS = 4
DIFF_HD = 64
DIFF_VD = 2 * DIFF_HD
SWA_HEADS = 16
SWA_KV_HEADS = 4
SWA_HD = 64
WINDOW = 128
BLOCK = 128
D_FF = 4 * D_MODEL
N_EVEN = (DEPTH + 1) // 2
N_ODD = DEPTH // 2
ALPHA = (2 * DEPTH) ** 0.25
BETA = (8 * DEPTH) ** -0.25
LN_EPS = 1e-5
RMS_EPS = 1e-6

EVEN_SIZES = [MLA_Q_LORA, MLA_KV_LORA, MLA_ROPE,
              DIFF_HEADS * 2 * DIFF_HD, DIFF_HEADS * 2 * DIFF_HD, DIFF_HEADS * DIFF_VD]
EVEN_IN = sum(EVEN_SIZES)
EVEN_OUT = MLA_HEADS * MLA_V + DIFF_HEADS * DIFF_VD
ODD_SIZES = [SWA_HEADS * SWA_HD, SWA_KV_HEADS * SWA_HD, SWA_KV_HEADS * SWA_HD]
ODD_IN = sum(ODD_SIZES)
ODD_OUT = SWA_HEADS * SWA_HD

kernel_name = "hybrid_mla_diff_swa_deepnorm_encoder"


def _offsets(sizes):
    return [int(v) for v in np.cumsum(sizes)[:-1]]


def layer_norm(x, g, b):
    xf = x.astype(jnp.float32)
    mu = jnp.mean(xf, axis=-1, keepdims=True)
    var = jnp.mean(jnp.square(xf - mu), axis=-1, keepdims=True)
    y = (xf - mu) * lax.rsqrt(var + LN_EPS) * g.astype(jnp.float32) + b.astype(jnp.float32)
    return y.astype(x.dtype)


def rms_norm(x, g):
    xf = x.astype(jnp.float32)
    y = xf * lax.rsqrt(jnp.mean(xf * xf, axis=-1, keepdims=True) + RMS_EPS)
    return (y * g.astype(jnp.float32)).astype(x.dtype)


def alibi_slopes(n):
    return 2.0 ** (-8.0 * jnp.arange(1, n + 1, dtype=jnp.float32) / n)


def rope(x, pos):
    half = x.shape[-1] // 2
    inv = ROPE_THETA ** (-jnp.arange(half, dtype=jnp.float32) / half)
    ang = pos.astype(jnp.float32)[:, None] * inv[None, :]
    cos = jnp.cos(ang)[:, None, :]
    sin = jnp.sin(ang)[:, None, :]
    x1 = x[..., :half].astype(jnp.float32)
    x2 = x[..., half:].astype(jnp.float32)
    return jnp.concatenate([x1 * cos - x2 * sin, x2 * cos + x1 * sin], axis=-1).astype(x.dtype)


def to_blocks(a):
    B, S = a.shape[:2]
    return a.reshape(B, S // BLOCK, BLOCK, *a.shape[2:]).swapaxes(0, 1)


def from_blocks(o):
    nb, B = o.shape[:2]
    return o.swapaxes(0, 1).reshape(B, nb * BLOCK, *o.shape[3:])


def mla_attention(q_nope, q_rope, k_nope, k_rope, v):
    scale = (MLA_NOPE + MLA_ROPE) ** -0.5

    def block(args):
        qn, qr = args
        s = (jnp.einsum('bqhd,bkhd->bhqk', qn, k_nope)
             + jnp.einsum('bqhr,bkr->bhqk', qr, k_rope)).astype(jnp.float32) * scale
        p = jax.nn.softmax(s, axis=-1)
        return jnp.einsum('bhqk,bkhd->bqhd', p.astype(v.dtype), v)

    return from_blocks(lax.map(block, (to_blocks(q_nope), to_blocks(q_rope))))


def diff_attention(q, k, v, lam):
    S = q.shape[1]
    scale = DIFF_HD ** -0.5
    slopes = alibi_slopes(DIFF_HEADS)
    kpos = jnp.arange(S)
    nb = S // BLOCK

    def block(args):
        i, qb = args
        qpos = i * BLOCK + jnp.arange(BLOCK)
        dist = jnp.abs(qpos[:, None] - kpos[None, :]).astype(jnp.float32)
        bias = -slopes[:, None, None, None] * dist[None, None]
        s = jnp.einsum('bqhmd,bkhmd->bhmqk', qb, k).astype(jnp.float32) * scale + bias
        p = jax.nn.softmax(s, axis=-1)
        a = p[:, :, 0] - lam * p[:, :, 1]
        return jnp.einsum('bhqk,bkhe->bqhe', a.astype(v.dtype), v)

    return from_blocks(lax.map(block, (jnp.arange(nb), to_blocks(q))))


def window_gqa_attention(q, k, v, sink):
    B, S, H, D = q.shape
    G = H // SWA_KV_HEADS
    nb = S // BLOCK
    span = BLOCK + 2 * WINDOW
    scale = D ** -0.5
    slopes = alibi_slopes(H).reshape(SWA_KV_HEADS, G)
    sink_f = sink.astype(jnp.float32).reshape(SWA_KV_HEADS, G, 1, 1)
    kp = jnp.pad(k, ((0, 0), (WINDOW, WINDOW), (0, 0), (0, 0)))
    vp = jnp.pad(v, ((0, 0), (WINDOW, WINDOW), (0, 0), (0, 0)))
    rel = jnp.arange(span)[None, :] - WINDOW - jnp.arange(BLOCK)[:, None]
    dist = jnp.abs(rel).astype(jnp.float32)

    def block(args):
        i, qb = args
        kb = lax.dynamic_slice_in_dim(kp, i * BLOCK, span, axis=1)
        vb = lax.dynamic_slice_in_dim(vp, i * BLOCK, span, axis=1)
        kpos = i * BLOCK - WINDOW + jnp.arange(span)
        valid = (jnp.abs(rel) <= WINDOW) & ((kpos >= 0) & (kpos < S))[None, :]
        qg = qb.reshape(B, BLOCK, SWA_KV_HEADS, G, D)
        s = jnp.einsum('bqngd,bsnd->bngqs', qg, kb).astype(jnp.float32) * scale
        s = s - slopes[:, :, None, None] * dist[None, None]
        s = jnp.where(valid[None, None, None], s, -jnp.inf)
        sink_col = jnp.broadcast_to(sink_f, (B, SWA_KV_HEADS, G, BLOCK, 1))
        p = jax.nn.softmax(jnp.concatenate([s, sink_col], axis=-1), axis=-1)[..., :-1]
        o = jnp.einsum('bngqs,bsnd->bqngd', p.astype(vb.dtype), vb)
        return o.reshape(B, BLOCK, H, D)

    return from_blocks(lax.map(block, (jnp.arange(nb), to_blocks(q))))


def even_mixer(x, w_in, q_norm, kv_norm, w_uq, w_ukv, lam_q1, lam_k1, lam_q2, lam_k2,
               diff_norm, w_out, lambda_init):
    B, S, _ = x.shape
    pos = jnp.arange(S)
    h = x @ w_in
    c_q, c_kv, k_rope, dq, dk, dv = jnp.split(h, _offsets(EVEN_SIZES), axis=-1)
    q = (rms_norm(c_q, q_norm) @ w_uq).reshape(B, S, MLA_HEADS, MLA_NOPE + MLA_ROPE)
    q_nope = q[..., :MLA_NOPE]
    q_rope = rope(q[..., MLA_NOPE:], pos)
    kv = (rms_norm(c_kv, kv_norm) @ w_ukv).reshape(B, S, MLA_HEADS, MLA_NOPE + MLA_V)
    k_nope = kv[..., :MLA_NOPE]
    v_mla = kv[..., MLA_NOPE:]
    k_rope = rope(k_rope[:, :, None, :], pos)[:, :, 0, :]
    o_mla = mla_attention(q_nope, q_rope, k_nope, k_rope, v_mla).reshape(B, S, MLA_HEADS * MLA_V)
    lam = (jnp.exp(jnp.sum(lam_q1.astype(jnp.float32) * lam_k1.astype(jnp.float32)))
           - jnp.exp(jnp.sum(lam_q2.astype(jnp.float32) * lam_k2.astype(jnp.float32)))
           + lambda_init)
    o_diff = diff_attention(dq.reshape(B, S, DIFF_HEADS, 2, DIFF_HD),
                            dk.reshape(B, S, DIFF_HEADS, 2, DIFF_HD),
                            dv.reshape(B, S, DIFF_HEADS, DIFF_VD), lam)
    o_diff = rms_norm(o_diff, diff_norm) * (1.0 - lambda_init)
    o = jnp.concatenate([o_mla, o_diff.reshape(B, S, DIFF_HEADS * DIFF_VD)], axis=-1)
    return o @ w_out


def odd_mixer(x, w_in, sink, w_out):
    B, S, _ = x.shape
    q, k, v = jnp.split(x @ w_in, _offsets(ODD_SIZES), axis=-1)
    o = window_gqa_attention(q.reshape(B, S, SWA_HEADS, SWA_HD),
                             k.reshape(B, S, SWA_KV_HEADS, SWA_HD),
                             v.reshape(B, S, SWA_KV_HEADS, SWA_HD), sink)
    return o.reshape(B, S, ODD_OUT) @ w_out


def sqrelu_mlp(x, w1, b1, w2, b2):
    return jnp.square(jax.nn.relu(x @ w1 + b1)) @ w2 + b2


def setup_inputs(seed: int = 0) -> dict:
    key = jax.random.key(seed)
    ks = jax.random.split(key, 24)

    def nrm(k, shape, scale):
        return jax.random.normal(k, shape, jnp.float32) * scale

    def gain(k, shape):
        return 1.0 + nrm(k, shape, 0.02)

    ev_cols = jnp.concatenate([jnp.ones((EVEN_IN - DIFF_HEADS * DIFF_VD,), jnp.float32),
                               jnp.full((DIFF_HEADS * DIFF_VD,), BETA, jnp.float32)])
    ukv_cols = jnp.tile(jnp.concatenate([jnp.ones((MLA_NOPE,), jnp.float32),
                                         jnp.full((MLA_V,), BETA, jnp.float32)]), MLA_HEADS)
    od_cols = jnp.concatenate([jnp.ones((ODD_IN - SWA_KV_HEADS * SWA_HD,), jnp.float32),
                               jnp.full((SWA_KV_HEADS * SWA_HD,), BETA, jnp.float32)])
    return {
        "x": nrm(ks[0], (BATCH, SEQ, D_MODEL), 1.0),
        "ev_w_in": nrm(ks[1], (N_EVEN, D_MODEL, EVEN_IN), D_MODEL ** -0.5) * ev_cols,
        "ev_q_norm": gain(ks[2], (N_EVEN, MLA_Q_LORA)),
        "ev_kv_norm": gain(ks[3], (N_EVEN, MLA_KV_LORA)),
        "ev_w_uq": nrm(ks[4], (N_EVEN, MLA_Q_LORA, MLA_HEADS * (MLA_NOPE + MLA_ROPE)), MLA_Q_LORA ** -0.5),
        "ev_w_ukv": nrm(ks[5], (N_EVEN, MLA_KV_LORA, MLA_HEADS * (MLA_NOPE + MLA_V)), MLA_KV_LORA ** -0.5) * ukv_cols,
        "ev_lam_q1": nrm(ks[6], (N_EVEN, DIFF_HD), 0.1),
        "ev_lam_k1": nrm(ks[7], (N_EVEN, DIFF_HD), 0.1),
        "ev_lam_q2": nrm(ks[8], (N_EVEN, DIFF_HD), 0.1),
        "ev_lam_k2": nrm(ks[9], (N_EVEN, DIFF_HD), 0.1),
        "ev_diff_norm": gain(ks[10], (N_EVEN, DIFF_VD)),
        "ev_w_out": nrm(ks[11], (N_EVEN, EVEN_OUT, D_MODEL), BETA * EVEN_OUT ** -0.5),
        "od_w_in": nrm(ks[12], (N_ODD, D_MODEL, ODD_IN), D_MODEL ** -0.5) * od_cols,
        "od_sink": nrm(ks[13], (N_ODD, SWA_HEADS), 0.5),
        "od_w_out": nrm(ks[14], (N_ODD, ODD_OUT, D_MODEL), BETA * ODD_OUT ** -0.5),
        "ln1_g": gain(ks[15], (DEPTH, D_MODEL)),
        "ln1_b": nrm(ks[16], (DEPTH, D_MODEL), 0.02),
        "ln2_g": gain(ks[17], (DEPTH, D_MODEL)),
        "ln2_b": nrm(ks[18], (DEPTH, D_MODEL), 0.02),
        "ffn_w1": nrm(ks[19], (DEPTH, D_MODEL, D_FF), D_MODEL ** -0.5),
        "ffn_b1": nrm(ks[20], (DEPTH, D_FF), 0.02),
        "ffn_w2": nrm(ks[21], (DEPTH, D_FF, D_MODEL), BETA * D_FF ** -0.5),
        "ffn_b2": nrm(ks[22], (DEPTH, D_MODEL), 0.02),
    }


def reference(x, ev_w_in, ev_q_norm, ev_kv_norm, ev_w_uq, ev_w_ukv, ev_lam_q1, ev_lam_k1,
              ev_lam_q2, ev_lam_k2, ev_diff_norm, ev_w_out, od_w_in, od_sink, od_w_out,
              ln1_g, ln1_b, ln2_g, ln2_b, ffn_w1, ffn_b1, ffn_w2, ffn_b2):
    for layer in range(DEPTH):
        j = layer // 2
        if layer % 2 == 0:
            lambda_init = 0.8 - 0.6 * math.exp(-0.3 * layer)
            y = even_mixer(x, ev_w_in[j], ev_q_norm[j], ev_kv_norm[j], ev_w_uq[j], ev_w_ukv[j],
                           ev_lam_q1[j], ev_lam_k1[j], ev_lam_q2[j], ev_lam_k2[j],
                           ev_diff_norm[j], ev_w_out[j], lambda_init)
        else:
            y = odd_mixer(x, od_w_in[j], od_sink[j], od_w_out[j])
        x = layer_norm(ALPHA * x + y, ln1_g[layer], ln1_b[layer])
        x = layer_norm(ALPHA * x + sqrelu_mlp(x, ffn_w1[layer], ffn_b1[layer], ffn_w2[layer], ffn_b2[layer]),
                       ln2_g[layer], ln2_b[layer])
    return x
```

```python
import functools
import math

import jax
import jax.numpy as jnp
import numpy as np
from jax import lax
from jax.experimental import pallas as pl
from jax.experimental.pallas import tpu as pltpu

F32 = jnp.float32
BF16 = jnp.bfloat16

D_MODEL = 1024
DEPTH = 2
MLA_HEADS = 8
MLA_NOPE = 64
MLA_ROPE = 32
MLA_V = 64
MLA_Q_LORA = 384
MLA_KV_LORA = 256
ROPE_THETA = 10000.0
DIFF_HEADS = 4
DIFF_HD = 64
DIFF_VD = 2 * DIFF_HD
SWA_HEADS = 16
SWA_KV_HEADS = 4
SWA_HD = 64
SWA_GROUP = SWA_HEADS // SWA_KV_HEADS
WINDOW = 128
BLOCK = 128
D_FF = 4 * D_MODEL
ALPHA = (2 * DEPTH) ** 0.25
LN_EPS = 1e-5
RMS_EPS = 1e-6

LANES = 128
HALF = LANES // 2
SPAN = BLOCK + 2 * WINDOW

_EV_CQ = 0
_EV_CKV = _EV_CQ + MLA_Q_LORA
_EV_KR = _EV_CKV + MLA_KV_LORA
_EV_DQ = _EV_KR + LANES
_EV_DK = _EV_DQ + DIFF_HEADS * LANES
_EV_DV = _EV_DK + DIFF_HEADS * LANES
_EV_END = _EV_DV + DIFF_HEADS * LANES

VMEM_LIMIT_BYTES = 56 * 1024 * 1024

TOKEN_TILE = 512
QUERY_TILE = 256
FF_CHUNK = 1024


def _params(*sem):
    return pltpu.CompilerParams(dimension_semantics=sem, vmem_limit_bytes=VMEM_LIMIT_BYTES)


def _dot(a, b):
    return jnp.dot(a, b, preferred_element_type=F32)


def _dot_nt(a, b):
    return lax.dot_general(a, b, (((1,), (1,)), ((), ())), preferred_element_type=F32)


def _rms(x, g):
    return x * lax.rsqrt(jnp.mean(x * x, axis=-1, keepdims=True) + RMS_EPS) * g


def _layer_norm(x, g, b):
    mu = jnp.mean(x, axis=-1, keepdims=True)
    xc = x - mu
    var = jnp.mean(xc * xc, axis=-1, keepdims=True)
    return xc * lax.rsqrt(var + LN_EPS) * g + b


def _rope_slab(blk, c, s_next, s_prev):
    nxt = pltpu.roll(blk, LANES - MLA_ROPE // 2, 1)
    prv = pltpu.roll(blk, MLA_ROPE // 2, 1)
    return blk * c + nxt * s_next + prv * s_prev


def _even_in_kernel(x_ref, w_in_ref, qn_ref, kvn_ref, w_uq_ref, w_ukv_ref, c_ref, sn_ref, sp_ref,
                    q_ref, k_ref, v_ref, dq_ref, dk_ref, dv_ref):
    xb = x_ref[0].astype(BF16)
    h = _dot(xb, w_in_ref[...])
    c, sn, sp = c_ref[...], sn_ref[...], sp_ref[...]

    cq = _rms(h[:, _EV_CQ:_EV_CKV], qn_ref[...])
    q = _dot(cq.astype(BF16), w_uq_ref[...])
    q_scale = (MLA_NOPE + MLA_ROPE) ** -0.5
    for hd in range(MLA_HEADS):
        blk = q[:, hd * LANES:(hd + 1) * LANES]
        q_ref[0, hd] = (_rope_slab(blk, c, sn, sp) * q_scale).astype(BF16)

    ckv = _rms(h[:, _EV_CKV:_EV_KR], kvn_ref[...])
    kv = _dot(ckv.astype(BF16), w_ukv_ref[...])
    kr = _rope_slab(h[:, _EV_KR:_EV_DQ], c, sn, sp)
    for hd in range(MLA_HEADS):
        k_ref[0, hd] = (kv[:, hd * LANES:(hd + 1) * LANES] + kr).astype(BF16)
    v_off = MLA_HEADS * LANES
    for pr in range(MLA_HEADS // 2):
        v_ref[0, pr] = kv[:, v_off + pr * LANES:v_off + (pr + 1) * LANES].astype(BF16)

    for hd in range(DIFF_HEADS):
        dq_ref[0, hd] = h[:, _EV_DQ + hd * LANES:_EV_DQ + (hd + 1) * LANES].astype(BF16)
        dk_ref[0, hd] = h[:, _EV_DK + hd * LANES:_EV_DK + (hd + 1) * LANES].astype(BF16)
        dv_ref[0, hd] = h[:, _EV_DV + hd * LANES:_EV_DV + (hd + 1) * LANES].astype(BF16)


def _even_in_proj(x, w_in_p, q_norm, kv_norm, w_uq_p, w_ukv_p, rope_c, rope_sn, rope_sp):
    B, S, D = x.shape
    tm = min(TOKEN_TILE, S)
    nt = S // tm
    const = lambda b, i: (0, 0)
    head_out = lambda n: pl.BlockSpec((1, n, tm, LANES), lambda b, i: (b, 0, i, 0))
    head_shape = lambda n: jax.ShapeDtypeStruct((B, n, S, LANES), BF16)
    return pl.pallas_call(
        _even_in_kernel,
        grid=(B, nt),
        in_specs=[
            pl.BlockSpec((1, tm, D), lambda b, i: (b, i, 0)),
            pl.BlockSpec(w_in_p.shape, const),
            pl.BlockSpec(q_norm.shape, const),
            pl.BlockSpec(kv_norm.shape, const),
            pl.BlockSpec(w_uq_p.shape, const),
            pl.BlockSpec(w_ukv_p.shape, const),
            pl.BlockSpec((tm, LANES), lambda b, i: (i, 0)),
            pl.BlockSpec((tm, LANES), lambda b, i: (i, 0)),
            pl.BlockSpec((tm, LANES), lambda b, i: (i, 0)),
        ],
        out_specs=[head_out(MLA_HEADS), head_out(MLA_HEADS), head_out(MLA_HEADS // 2),
                   head_out(DIFF_HEADS), head_out(DIFF_HEADS), head_out(DIFF_HEADS)],
        out_shape=[head_shape(MLA_HEADS), head_shape(MLA_HEADS), head_shape(MLA_HEADS // 2),
                   head_shape(DIFF_HEADS), head_shape(DIFF_HEADS), head_shape(DIFF_HEADS)],
        compiler_params=_params("parallel", "parallel"),
        name="even_in_proj",
    )(x, w_in_p, q_norm, kv_norm, w_uq_p, w_ukv_p, rope_c, rope_sn, rope_sp)


def _mla_attn_kernel(q_ref, k_ref, v_ref, o_ref):
    v = v_ref[0, 0]
    outs = []
    for hh in range(2):
        s = _dot_nt(q_ref[0, hh], k_ref[0, hh])
        m = jnp.max(s, axis=-1, keepdims=True)
        p = jnp.exp(s - m)
        l = jnp.sum(p, axis=-1, keepdims=True)
        outs.append(_dot(p.astype(BF16), v) / l)
    lane = lax.broadcasted_iota(jnp.int32, outs[0].shape, 1)
    o_ref[0] = jnp.where(lane < HALF, outs[0], outs[1]).astype(o_ref.dtype)


def _mla_attention(q, k, v):
    B, H, S, _ = q.shape
    tq = min(QUERY_TILE, S)
    return pl.pallas_call(
        _mla_attn_kernel,
        grid=(B, H // 2, S // tq),
        in_specs=[
            pl.BlockSpec((1, 2, tq, LANES), lambda b, j, i: (b, j, i, 0)),
            pl.BlockSpec((1, 2, S, LANES), lambda b, j, i: (b, j, 0, 0)),
            pl.BlockSpec((1, 1, S, LANES), lambda b, j, i: (b, j, 0, 0)),
        ],
        out_specs=pl.BlockSpec((1, tq, LANES), lambda b, j, i: (b, i, j)),
        out_shape=jax.ShapeDtypeStruct((B, S, H * MLA_V), BF16),
        compiler_params=_params("parallel", "parallel", "arbitrary"),
        name="mla_attention",
    )(q, k, v)


def _diff_attn_kernel(slope_ref, lam_ref, q_ref, k_ref, v_ref, g_ref, o_ref, *, lambda_init):
    hd = pl.program_id(1)
    qi = pl.program_id(2)
    q, k, v = q_ref[0, 0], k_ref[0, 0], v_ref[0, 0]
    tq, S = q.shape[0], k.shape[0]

    lv = lam_ref[...]
    lam = (jnp.exp(jnp.sum(lv[0:1] * lv[1:2], axis=-1, keepdims=True))
           - jnp.exp(jnp.sum(lv[2:3] * lv[3:4], axis=-1, keepdims=True)) + lambda_init)

    qpos = qi * tq + lax.broadcasted_iota(jnp.int32, (tq, 1), 0)
    kpos = lax.broadcasted_iota(jnp.int32, (1, S), 1)
    bias = jnp.abs(qpos - kpos).astype(F32) * (-slope_ref[hd])

    lane = lax.broadcasted_iota(jnp.int32, q.shape, 1)
    zero = jnp.zeros_like(q)
    outs = []
    for q_m in (jnp.where(lane < HALF, q, zero), jnp.where(lane >= HALF, q, zero)):
        s = _dot_nt(q_m, k) + bias
        m = jnp.max(s, axis=-1, keepdims=True)
        p = jnp.exp(s - m)
        l = jnp.sum(p, axis=-1, keepdims=True)
        outs.append(_dot(p.astype(BF16), v) / l)
    o = outs[0] - lam * outs[1]
    o = _rms(o, g_ref[...]) * (1.0 - lambda_init)
    o_ref[0] = o.astype(o_ref.dtype)


def _diff_attention(slopes, lam_vecs, q, k, v, diff_norm, lambda_init):
    B, H, S, _ = q.shape
    tq = min(QUERY_TILE, S)
    smem = pl.BlockSpec(memory_space=pltpu.SMEM)
    return pl.pallas_call(
        functools.partial(_diff_attn_kernel, lambda_init=lambda_init),
        grid=(B, H, S // tq),
        in_specs=[
            smem,
            pl.BlockSpec(lam_vecs.shape, lambda b, h, i: (0, 0)),
            pl.BlockSpec((1, 1, tq, LANES), lambda b, h, i: (b, h, i, 0)),
            pl.BlockSpec((1, 1, S, LANES), lambda b, h, i: (b, h, 0, 0)),
            pl.BlockSpec((1, 1, S, LANES), lambda b, h, i: (b, h, 0, 0)),
            pl.BlockSpec(diff_norm.shape, lambda b, h, i: (0, 0)),
        ],
        out_specs=pl.BlockSpec((1, tq, LANES), lambda b, h, i: (b, i, h)),
        out_shape=jax.ShapeDtypeStruct((B, S, H * DIFF_VD), BF16),
        compiler_params=_params("parallel", "parallel", "arbitrary"),
        name="diff_attention",
    )(slopes, lam_vecs, q, k, v, diff_norm)


def _odd_in_kernel(x_ref, w_ref, q_ref, k_ref, v_ref):
    h = _dot(x_ref[0].astype(BF16), w_ref[...])
    nq = SWA_HEADS // 2
    for pr in range(nq):
        q_ref[0, pr] = h[:, pr * LANES:(pr + 1) * LANES].astype(BF16)
    for g in range(SWA_KV_HEADS):
        k_ref[0, g] = h[:, (nq + g) * LANES:(nq + g + 1) * LANES].astype(BF16)
        v_ref[0, g] = h[:, (nq + SWA_KV_HEADS + g) * LANES:(nq + SWA_KV_HEADS + g + 1) * LANES].astype(BF16)


def _odd_in_proj(x, w_p):
    B, S, D = x.shape
    tm = min(TOKEN_TILE, S)
    head_out = lambda n: pl.BlockSpec((1, n, tm, LANES), lambda b, i: (b, 0, i, 0))
    head_shape = lambda n: jax.ShapeDtypeStruct((B, n, S, LANES), BF16)
    return pl.pallas_call(
        _odd_in_kernel,
        grid=(B, S // tm),
        in_specs=[
            pl.BlockSpec((1, tm, D), lambda b, i: (b, i, 0)),
            pl.BlockSpec(w_p.shape, lambda b, i: (0, 0)),
        ],
        out_specs=[head_out(SWA_HEADS // 2), head_out(SWA_KV_HEADS), head_out(SWA_KV_HEADS)],
        out_shape=[head_shape(SWA_HEADS // 2), head_shape(SWA_KV_HEADS), head_shape(SWA_KV_HEADS)],
        compiler_params=_params("parallel", "parallel"),
        name="odd_in_proj",
    )(x, w_p)


def _swa_attn_kernel(slope_ref, sink_ref, q_ref, k_ref, v_ref, o_ref):
    g = pl.program_id(1)
    S = k_ref.shape[2]
    rows = SWA_GROUP * BLOCK
    lane = lax.broadcasted_iota(jnp.int32, (BLOCK, LANES), 1)
    low = lane < HALF
    head_of_row = lax.broadcasted_iota(jnp.int32, (rows, 1), 0) // BLOCK
    slope_col = jnp.zeros((rows, 1), F32)
    sink_col = jnp.zeros((rows, 1), F32)
    for hh in range(SWA_GROUP):
        slope_col = jnp.where(head_of_row == hh, slope_ref[g * SWA_GROUP + hh], slope_col)
        sink_col = jnp.where(head_of_row == hh, sink_ref[g * SWA_GROUP + hh], sink_col)

    def block(i, carry):
        q0 = pl.multiple_of(i * BLOCK, BLOCK)
        ws = pl.multiple_of(jnp.clip(q0 - WINDOW, 0, S - SPAN), BLOCK)
        kb = k_ref[0, 0, pl.ds(ws, SPAN), :]
        vb = v_ref[0, 0, pl.ds(ws, SPAN), :]
        stacked = []
        for pr in range(2):
            qp = q_ref[0, pr, pl.ds(q0, BLOCK), :]
            zero = jnp.zeros_like(qp)
            stacked.append(jnp.where(low, qp, zero))
            stacked.append(jnp.where(low, zero, qp))
        qs = jnp.concatenate(stacked, axis=0)
        qpos = q0 + lax.broadcasted_iota(jnp.int32, (rows, 1), 0) % BLOCK
        kpos = ws + lax.broadcasted_iota(jnp.int32, (1, SPAN), 1)
        dist = jnp.abs(kpos - qpos)
        s = _dot_nt(qs, kb) - slope_col * dist.astype(F32)
        s = jnp.where(dist <= WINDOW, s, -jnp.inf)
        m = jnp.maximum(jnp.max(s, axis=-1, keepdims=True), sink_col)
        p = jnp.exp(s - m)
        l = jnp.sum(p, axis=-1, keepdims=True) + jnp.exp(sink_col - m)
        o = _dot(p.astype(BF16), vb) / l
        for pr in range(2):
            o_pair = jnp.where(low, o[(2 * pr) * BLOCK:(2 * pr + 1) * BLOCK],
                               o[(2 * pr + 1) * BLOCK:(2 * pr + 2) * BLOCK])
            o_ref[0, pl.ds(q0, BLOCK), pr * LANES:(pr + 1) * LANES] = o_pair.astype(o_ref.dtype)
        return carry

    lax.fori_loop(0, S // BLOCK, block, 0)


def _swa_attention(slopes, sink, q, k, v):
    B, _, S, _ = q.shape
    smem = pl.BlockSpec(memory_space=pltpu.SMEM)
    return pl.pallas_call(
        _swa_attn_kernel,
        grid=(B, SWA_KV_HEADS),
        in_specs=[
            smem, smem,
            pl.BlockSpec((1, 2, S, LANES), lambda b, g: (b, g, 0, 0)),
            pl.BlockSpec((1, 1, S, LANES), lambda b, g: (b, g, 0, 0)),
            pl.BlockSpec((1, 1, S, LANES), lambda b, g: (b, g, 0, 0)),
        ],
        out_specs=pl.BlockSpec((1, S, 2 * LANES), lambda b, g: (b, 0, g)),
        out_shape=jax.ShapeDtypeStruct((B, S, SWA_HEADS * SWA_HD), BF16),
        compiler_params=_params("parallel", "parallel"),
        name="swa_attention",
    )(slopes, sink, q, k, v)


def _post_kernel(oa_ref, ob_ref, x_ref, woa_ref, wob_ref, g1_ref, b1_ref,
                 w1_ref, fb1_ref, w2_ref, fb2_ref, g2_ref, b2_ref, out_ref):
    y = _dot(oa_ref[...], woa_ref[...]) + _dot(ob_ref[...], wob_ref[...])
    x1 = _layer_norm(ALPHA * x_ref[...] + y, g1_ref[...], b1_ref[...])
    x1b = x1.astype(BF16)
    acc = jnp.zeros_like(x1) + fb2_ref[...]
    for c in range(0, w1_ref.shape[1], FF_CHUNK):
        hc = _dot(x1b, w1_ref[:, c:c + FF_CHUNK]) + fb1_ref[:, c:c + FF_CHUNK]
        hc = jnp.square(jnp.maximum(hc, 0.0))
        acc = acc + _dot(hc.astype(BF16), w2_ref[c:c + FF_CHUNK, :])
    out_ref[...] = _layer_norm(ALPHA * x1 + acc, g2_ref[...], b2_ref[...])


def _post(o_a, o_b, a_blk, b_blk, x, wo_a, wo_b, g1, b1, w1, fb1, w2, fb2, g2, b2):
    T, D = x.shape
    tm = min(TOKEN_TILE, T)
    half = wo_a.shape[0]
    const = lambda i: (0, 0)
    resident = lambda a: pl.BlockSpec(a.shape, const, pipeline_mode=pl.Buffered(1))
    return pl.pallas_call(
        _post_kernel,
        grid=(T // tm,),
        in_specs=[
            pl.BlockSpec((tm, half), lambda i: (i, a_blk)),
            pl.BlockSpec((tm, half), lambda i: (i, b_blk)),
            pl.BlockSpec((tm, D), lambda i: (i, 0)),
            resident(wo_a), resident(wo_b), resident(g1), resident(b1),
            resident(w1), resident(fb1), resident(w2), resident(fb2), resident(g2), resident(b2),
        ],
        out_specs=pl.BlockSpec((tm, D), lambda i: (i, 0)),
        out_shape=jax.ShapeDtypeStruct((T, D), F32),
        compiler_params=_params("parallel"),
        name="post_mlp",
    )(o_a, o_b, x, wo_a, wo_b, g1, b1, w1, fb1, w2, fb2, g2, b2)


def _prep_even_weights(w_in, w_uq, w_ukv):
    D = w_in.shape[0]
    o_kr = MLA_Q_LORA + MLA_KV_LORA
    o_dq = o_kr + MLA_ROPE
    n_diff = DIFF_HEADS * 2 * DIFF_HD
    kr_slab = jnp.zeros((D, LANES), F32).at[:, MLA_NOPE:MLA_NOPE + MLA_ROPE].set(w_in[:, o_kr:o_dq])
    w_in_p = jnp.concatenate([
        w_in[:, :o_kr], kr_slab,
        w_in[:, o_dq:o_dq + n_diff] * (DIFF_HD ** -0.5),
        w_in[:, o_dq + n_diff:]], axis=1).astype(BF16)
    assert w_in_p.shape[1] == _EV_END

    r = w_uq.shape[0]
    qh = w_uq.reshape(r, MLA_HEADS, MLA_NOPE + MLA_ROPE)
    qh = jnp.pad(qh, ((0, 0), (0, 0), (0, LANES - MLA_NOPE - MLA_ROPE)))
    w_uq_p = qh.reshape(r, MLA_HEADS * LANES).astype(BF16)

    r = w_ukv.shape[0]
    kvh = w_ukv.reshape(r, MLA_HEADS, MLA_NOPE + MLA_V)
    kn = jnp.pad(kvh[:, :, :MLA_NOPE], ((0, 0), (0, 0), (0, LANES - MLA_NOPE)))
    w_ukv_p = jnp.concatenate([kn.reshape(r, MLA_HEADS * LANES),
                               kvh[:, :, MLA_NOPE:].reshape(r, MLA_HEADS * MLA_V)], axis=1).astype(BF16)
    return w_in_p, w_uq_p, w_ukv_p


def _prep_odd_weights(w_in):
    D = w_in.shape[0]
    nq = SWA_HEADS * SWA_HD
    nkv = SWA_KV_HEADS * SWA_HD
    wq = w_in[:, :nq] * (SWA_HD ** -0.5)
    dup = lambda w: jnp.concatenate([w.reshape(D, SWA_KV_HEADS, SWA_HD)] * 2, axis=2).reshape(D, 2 * nkv)
    return jnp.concatenate([wq, dup(w_in[:, nq:nq + nkv]), dup(w_in[:, nq + nkv:])], axis=1).astype(BF16)


def _rope_tables(S):
    half = MLA_ROPE // 2
    inv = ROPE_THETA ** (-np.arange(half, dtype=np.float32) / half)
    ang = np.arange(S, dtype=np.float32)[:, None] * inv[None, :]
    cos, sin = np.cos(ang).astype(np.float32), np.sin(ang).astype(np.float32)
    c = np.zeros((S, LANES), np.float32)
    sn = np.zeros((S, LANES), np.float32)
    sp = np.zeros((S, LANES), np.float32)
    c[:, :MLA_NOPE] = 1.0
    c[:, MLA_NOPE:MLA_NOPE + half] = cos
    c[:, MLA_NOPE + half:MLA_NOPE + 2 * half] = cos
    sn[:, MLA_NOPE:MLA_NOPE + half] = -sin
    sp[:, MLA_NOPE + half:MLA_NOPE + 2 * half] = sin
    return jnp.asarray(c), jnp.asarray(sn), jnp.asarray(sp)


def _alibi_slopes(n):
    return jnp.asarray(2.0 ** (-8.0 * np.arange(1, n + 1, dtype=np.float32) / n), F32)


def kernel(x, ev_w_in, ev_q_norm, ev_kv_norm, ev_w_uq, ev_w_ukv, ev_lam_q1, ev_lam_k1, ev_lam_q2, ev_lam_k2, ev_diff_norm, ev_w_out, od_w_in, od_sink, od_w_out, ln1_g, ln1_b, ln2_g, ln2_b, ffn_w1, ffn_b1, ffn_w2, ffn_b2):
    B, S, D = x.shape
    row = lambda a: a.reshape(1, -1).astype(F32)
    rope_c, rope_sn, rope_sp = _rope_tables(S)
    xs = x
    for layer in range(DEPTH):
        j = layer // 2
        if layer % 2 == 0:
            lambda_init = 0.8 - 0.6 * math.exp(-0.3 * layer)
            w_in_p, w_uq_p, w_ukv_p = _prep_even_weights(ev_w_in[j], ev_w_uq[j], ev_w_ukv[j])
            q, k, v, dq, dk, dv = _even_in_proj(xs, w_in_p, row(ev_q_norm[j]), row(ev_kv_norm[j]),
                                                w_uq_p, w_ukv_p, rope_c, rope_sn, rope_sp)
            o_mla = _mla_attention(q, k, v)
            lam_vecs = jnp.stack([ev_lam_q1[j], ev_lam_k1[j], ev_lam_q2[j], ev_lam_k2[j]]).astype(F32)
            o_diff = _diff_attention(_alibi_slopes(DIFF_HEADS), lam_vecs, dq, dk, dv,
                                     row(ev_diff_norm[j]), lambda_init)
            n_a = o_mla.shape[-1]
            o_a, o_b, a_blk, b_blk = o_mla.reshape(B * S, n_a), o_diff.reshape(B * S, -1), 0, 0
            w_out = ev_w_out[j]
        else:
            q, k, v = _odd_in_proj(xs, _prep_odd_weights(od_w_in[j]))
            o = _swa_attention(_alibi_slopes(SWA_HEADS), od_sink[j].astype(F32), q, k, v)
            n_a = o.shape[-1] // 2
            o_a = o_b = o.reshape(B * S, -1)
            a_blk, b_blk = 0, 1
            w_out = od_w_out[j]
        xs = _post(o_a, o_b, a_blk, b_blk, xs.reshape(B * S, D),
                   w_out[:n_a].astype(BF16), w_out[n_a:].astype(BF16),
                   row(ln1_g[layer]), row(ln1_b[layer]),
                   ffn_w1[layer].astype(BF16), row(ffn_b1[layer]),
                   ffn_w2[layer].astype(BF16), row(ffn_b2[layer]),
                   row(ln2_g[layer]), row(ln2_b[layer])).reshape(B, S, D)
    return xs
```

```python
import functools
import math

import jax
import jax.numpy as jnp
import numpy as np
from jax import lax
from jax.experimental import pallas as pl
from jax.experimental.pallas import tpu as pltpu

F32 = jnp.float32
BF16 = jnp.bfloat16

D_MODEL = 1024
DEPTH = 2
MLA_HEADS = 8
MLA_NOPE = 64
MLA_ROPE = 32
MLA_V = 64
MLA_Q_LORA = 384
MLA_KV_LORA = 256
ROPE_THETA = 10000.0
DIFF_HEADS = 4
DIFF_HD = 64
DIFF_VD = 2 * DIFF_HD
SWA_HEADS = 16
SWA_KV_HEADS = 4
SWA_HD = 64
SWA_GROUP = SWA_HEADS // SWA_KV_HEADS
WINDOW = 128
BLOCK = 128
D_FF = 4 * D_MODEL
ALPHA = (2 * DEPTH) ** 0.25
LN_EPS = 1e-5
RMS_EPS = 1e-6
LOG2E = math.log2(math.e)

LANES = 128
HALF = LANES // 2
SPAN = BLOCK + 2 * WINDOW

_EV_CQ = 0
_EV_CKV = _EV_CQ + MLA_Q_LORA
_EV_KR = _EV_CKV + MLA_KV_LORA
_EV_DQ = _EV_KR + LANES
_EV_DK = _EV_DQ + DIFF_HEADS * LANES
_EV_DV = _EV_DK + DIFF_HEADS * LANES
_EV_END = _EV_DV + DIFF_HEADS * LANES

VMEM_LIMIT_BYTES = 56 * 1024 * 1024

TOKEN_TILE = 512
QUERY_TILE = 256
KEY_CHUNK = 512
FF_CHUNK = 1024


def _params(*sem):
    return pltpu.CompilerParams(dimension_semantics=sem, vmem_limit_bytes=VMEM_LIMIT_BYTES)


def _dot(a, b):
    return jnp.dot(a, b, preferred_element_type=F32)


def _dot_nt(a, b):
    return lax.dot_general(a, b, (((1,), (1,)), ((), ())), preferred_element_type=F32)


def _rms(x, g):
    return x * lax.rsqrt(jnp.mean(x * x, axis=-1, keepdims=True) + RMS_EPS) * g


def _layer_norm(x, g, b):
    mu = jnp.mean(x, axis=-1, keepdims=True)
    xc = x - mu
    var = jnp.mean(xc * xc, axis=-1, keepdims=True)
    return xc * lax.rsqrt(var + LN_EPS) * g + b


def _rope_slab(blk, c, s_next, s_prev):
    nxt = pltpu.roll(blk, LANES - MLA_ROPE // 2, 1)
    prv = pltpu.roll(blk, MLA_ROPE // 2, 1)
    return blk * c + nxt * s_next + prv * s_prev


def _softmax_update(s, v, m, acc):
    m_new = jnp.max(s, axis=-1, keepdims=True)
    if m is not None:
        m_new = jnp.maximum(m, m_new)
    p = jnp.exp2((s - m_new).astype(BF16))
    pv = _dot(p, v)
    if m is not None:
        pv = acc * jnp.exp2(m - m_new) + pv
    return m_new, pv


def _even_in_kernel(x_ref, w_in_ref, qn_ref, kvn_ref, w_uq_ref, w_ukv_ref, c_ref, sn_ref, sp_ref,
                    q_ref, k_ref, v_ref, dq_ref, dk_ref, dv_ref):
    xb = x_ref[0].astype(BF16)
    h = _dot(xb, w_in_ref[...])
    c, sn, sp = c_ref[...], sn_ref[...], sp_ref[...]
    low = lax.broadcasted_iota(jnp.int32, (xb.shape[0], LANES), 1) < HALF

    cq = _rms(h[:, _EV_CQ:_EV_CKV], qn_ref[...])
    q = _dot(cq.astype(BF16), w_uq_ref[...])
    q_scale = (MLA_NOPE + MLA_ROPE) ** -0.5 * LOG2E
    for hd in range(MLA_HEADS):
        blk = q[:, hd * LANES:(hd + 1) * LANES]
        q_ref[0, hd] = (_rope_slab(blk, c, sn, sp) * q_scale).astype(BF16)

    ckv = _rms(h[:, _EV_CKV:_EV_KR], kvn_ref[...])
    kv = _dot(ckv.astype(BF16), w_ukv_ref[...])
    kr = _rope_slab(h[:, _EV_KR:_EV_DQ], c, sn, sp)
    v_off = MLA_HEADS * LANES
    for hd in range(MLA_HEADS):
        k_ref[0, hd] = (kv[:, hd * LANES:(hd + 1) * LANES] + kr).astype(BF16)
        vh = kv[:, v_off + hd * LANES:v_off + (hd + 1) * LANES]
        v_ref[0, hd] = jnp.where(low, vh, 1.0).astype(BF16)

    dq_scale = DIFF_HD ** -0.5 * LOG2E
    ones = jnp.ones((xb.shape[0], LANES), BF16)
    for hd in range(DIFF_HEADS):
        dq_ref[0, hd] = (h[:, _EV_DQ + hd * LANES:_EV_DQ + (hd + 1) * LANES] * dq_scale).astype(BF16)
        dk_ref[0, hd] = h[:, _EV_DK + hd * LANES:_EV_DK + (hd + 1) * LANES].astype(BF16)
        dv_ref[0, hd, :, :LANES] = h[:, _EV_DV + hd * LANES:_EV_DV + (hd + 1) * LANES].astype(BF16)
        dv_ref[0, hd, :, LANES:] = ones


def _even_in_proj(x, w_in_p, q_norm, kv_norm, w_uq_p, w_ukv_p, rope_c, rope_sn, rope_sp):
    B, S, D = x.shape
    tm = min(TOKEN_TILE, S)
    nt = S // tm
    const = lambda b, i: (0, 0)
    head_out = lambda n, w=LANES: pl.BlockSpec((1, n, tm, w), lambda b, i: (b, 0, i, 0))
    head_shape = lambda n, w=LANES: jax.ShapeDtypeStruct((B, n, S, w), BF16)
    return pl.pallas_call(
        _even_in_kernel,
        grid=(B, nt),
        in_specs=[
            pl.BlockSpec((1, tm, D), lambda b, i: (b, i, 0)),
            pl.BlockSpec(w_in_p.shape, const),
            pl.BlockSpec(q_norm.shape, const),
            pl.BlockSpec(kv_norm.shape, const),
            pl.BlockSpec(w_uq_p.shape, const),
            pl.BlockSpec(w_ukv_p.shape, const),
            pl.BlockSpec((tm, LANES), lambda b, i: (i, 0)),
            pl.BlockSpec((tm, LANES), lambda b, i: (i, 0)),
            pl.BlockSpec((tm, LANES), lambda b, i: (i, 0)),
        ],
        out_specs=[head_out(MLA_HEADS), head_out(MLA_HEADS), head_out(MLA_HEADS),
                   head_out(DIFF_HEADS), head_out(DIFF_HEADS), head_out(DIFF_HEADS, 2 * LANES)],
        out_shape=[head_shape(MLA_HEADS), head_shape(MLA_HEADS), head_shape(MLA_HEADS),
                   head_shape(DIFF_HEADS), head_shape(DIFF_HEADS), head_shape(DIFF_HEADS, 2 * LANES)],
        compiler_params=_params("parallel", "parallel"),
        name="even_in_proj",
    )(x, w_in_p, q_norm, kv_norm, w_uq_p, w_ukv_p, rope_c, rope_sn, rope_sp)


def _mla_attn_kernel(q_ref, k_ref, v_ref, o_ref):
    S = k_ref.shape[2]
    kc = min(KEY_CHUNK, S)
    accs = []
    for hh in range(2):
        s = _dot_nt(q_ref[0, hh], k_ref[0, hh])
        _, acc = _softmax_update(s, v_ref[0, hh], None, None)
        accs.append(acc)
    low = lax.broadcasted_iota(jnp.int32, accs[0].shape, 1) < HALF
    sums0 = pltpu.roll(accs[0], HALF, 1)
    vals1 = pltpu.roll(accs[1], HALF, 1)
    o_ref[0] = jnp.where(low, accs[0] / sums0, vals1 / accs[1]).astype(o_ref.dtype)


def _mla_attention(q, k, v):
    B, H, S, _ = q.shape
    tq = min(QUERY_TILE, S)
    return pl.pallas_call(
        _mla_attn_kernel,
        grid=(B, H // 2, S // tq),
        in_specs=[
            pl.BlockSpec((1, 2, tq, LANES), lambda b, j, i: (b, j, i, 0)),
            pl.BlockSpec((1, 2, S, LANES), lambda b, j, i: (b, j, 0, 0)),
            pl.BlockSpec((1, 2, S, LANES), lambda b, j, i: (b, j, 0, 0)),
        ],
        out_specs=pl.BlockSpec((1, tq, LANES), lambda b, j, i: (b, i, j)),
        out_shape=jax.ShapeDtypeStruct((B, S, H * MLA_V), BF16),
        compiler_params=_params("parallel", "parallel", "arbitrary"),
        name="mla_attention",
    )(q, k, v)


def _diff_attn_kernel(slope_ref, lam_ref, q_ref, k_ref, v_ref, g_ref, o_ref, *, lambda_init):
    hd = pl.program_id(1)
    qi = pl.program_id(2)
    q = q_ref[0, 0]
    tq, S = q.shape[0], k_ref.shape[2]
    kc = min(KEY_CHUNK, S)

    lv = lam_ref[...]
    lam = (jnp.exp(jnp.sum(lv[0:1] * lv[1:2], axis=-1, keepdims=True))
           - jnp.exp(jnp.sum(lv[2:3] * lv[3:4], axis=-1, keepdims=True)) + lambda_init)

    slope = slope_ref[hd]
    q_pos = (qi * tq + lax.broadcasted_iota(jnp.int32, (tq, 1), 0)).astype(F32) * slope
    lane = lax.broadcasted_iota(jnp.int32, q.shape, 1)
    zero = jnp.zeros_like(q)
    q_maps = (jnp.where(lane < HALF, q, zero), jnp.where(lane >= HALF, q, zero))
    state = [(None, None), (None, None)]
    for c in range(0, S, kc):
        k = k_ref[0, 0, c:c + kc, :]
        v = v_ref[0, 0, c:c + kc, :]
        k_pos = (c + lax.broadcasted_iota(jnp.int32, (1, kc), 1)).astype(F32) * slope
        dist = jnp.abs(k_pos - q_pos)
        for mp in range(2):
            s = _dot_nt(q_maps[mp], k) - dist
            state[mp] = _softmax_update(s, v, *state[mp])
    outs = [acc[:, :LANES] / acc[:, LANES:] for _, acc in state]
    o = outs[0] - lam * outs[1]
    o = _rms(o, g_ref[...]) * (1.0 - lambda_init)
    o_ref[0] = o.astype(o_ref.dtype)


def _diff_attention(slopes, lam_vecs, q, k, v, diff_norm, lambda_init):
    B, H, S, _ = q.shape
    tq = min(QUERY_TILE, S)
    smem = pl.BlockSpec(memory_space=pltpu.SMEM)
    return pl.pallas_call(
        functools.partial(_diff_attn_kernel, lambda_init=lambda_init),
        grid=(B, H, S // tq),
        in_specs=[
            smem,
            pl.BlockSpec(lam_vecs.shape, lambda b, h, i: (0, 0)),
            pl.BlockSpec((1, 1, tq, LANES), lambda b, h, i: (b, h, i, 0)),
            pl.BlockSpec((1, 1, S, LANES), lambda b, h, i: (b, h, 0, 0)),
            pl.BlockSpec((1, 1, S, 2 * LANES), lambda b, h, i: (b, h, 0, 0)),
            pl.BlockSpec(diff_norm.shape, lambda b, h, i: (0, 0)),
        ],
        out_specs=pl.BlockSpec((1, tq, LANES), lambda b, h, i: (b, i, h)),
        out_shape=jax.ShapeDtypeStruct((B, S, H * DIFF_VD), BF16),
        compiler_params=_params("parallel", "parallel", "arbitrary"),
        name="diff_attention",
    )(slopes, lam_vecs, q, k, v, diff_norm)


def _odd_in_kernel(x_ref, w_ref, q_ref, k_ref, v_ref):
    h = _dot(x_ref[0].astype(BF16), w_ref[...])
    nq = SWA_HEADS // 2
    q_scale = SWA_HD ** -0.5 * LOG2E
    ones = jnp.ones((h.shape[0], LANES), BF16)
    for pr in range(nq):
        q_ref[0, pr] = (h[:, pr * LANES:(pr + 1) * LANES] * q_scale).astype(BF16)
    for g in range(SWA_KV_HEADS):
        k_ref[0, g] = h[:, (nq + g) * LANES:(nq + g + 1) * LANES].astype(BF16)
        v_lo = (nq + SWA_KV_HEADS + g) * LANES
        v_ref[0, g, :, :LANES] = h[:, v_lo:v_lo + LANES].astype(BF16)
        v_ref[0, g, :, LANES:] = ones


def _odd_in_proj(x, w_p):
    B, S, D = x.shape
    tm = min(TOKEN_TILE, S)
    head_out = lambda n, w=LANES: pl.BlockSpec((1, n, tm, w), lambda b, i: (b, 0, i, 0))
    head_shape = lambda n, w=LANES: jax.ShapeDtypeStruct((B, n, S, w), BF16)
    return pl.pallas_call(
        _odd_in_kernel,
        grid=(B, S // tm),
        in_specs=[
            pl.BlockSpec((1, tm, D), lambda b, i: (b, i, 0)),
            pl.BlockSpec(w_p.shape, lambda b, i: (0, 0)),
        ],
        out_specs=[head_out(SWA_HEADS // 2), head_out(SWA_KV_HEADS), head_out(SWA_KV_HEADS, 2 * LANES)],
        out_shape=[head_shape(SWA_HEADS // 2), head_shape(SWA_KV_HEADS),
                   head_shape(SWA_KV_HEADS, 2 * LANES)],
        compiler_params=_params("parallel", "parallel"),
        name="odd_in_proj",
    )(x, w_p)


_SWA_SHIFTS = (-WINDOW, 0, -2 * WINDOW)


def _swa_attn_kernel(slope_ref, sink_ref, q_ref, k_ref, v_ref, o_ref, bias_ref):
    g = pl.program_id(1)
    S = k_ref.shape[2]
    nb = S // BLOCK
    rows = SWA_GROUP * BLOCK
    low = lax.broadcasted_iota(jnp.int32, (BLOCK, LANES), 1) < HALF
    head_of_row = lax.broadcasted_iota(jnp.int32, (rows, 1), 0) // BLOCK
    slope_col = jnp.zeros((rows, 1), F32)
    sink_col = jnp.zeros((rows, 1), F32)
    for hh in range(SWA_GROUP):
        slope_col = jnp.where(head_of_row == hh, slope_ref[g * SWA_GROUP + hh], slope_col)
        sink_col = jnp.where(head_of_row == hh, sink_ref[g * SWA_GROUP + hh] * LOG2E, sink_col)

    rel0 = (lax.broadcasted_iota(jnp.int32, (rows, SPAN), 1)
            - lax.broadcasted_iota(jnp.int32, (rows, SPAN), 0) % BLOCK)
    for t, shift in enumerate(_SWA_SHIFTS):
        dist = jnp.abs(rel0 + shift)
        bias_ref[t] = jnp.where(dist <= WINDOW, -slope_col * dist.astype(F32), -jnp.inf)

    def block(i, carry):
        q0 = pl.multiple_of(i * BLOCK, BLOCK)
        ws = pl.multiple_of(jnp.clip(q0 - WINDOW, 0, S - SPAN), BLOCK)
        placement = jnp.where(i == 0, 1, jnp.where(i == nb - 1, 2, 0))
        kb = k_ref[0, 0, pl.ds(ws, SPAN), :]
        vb = v_ref[0, 0, pl.ds(ws, SPAN), :]
        stacked = []
        for pr in range(2):
            qp = q_ref[0, pr, pl.ds(q0, BLOCK), :]
            zero = jnp.zeros_like(qp)
            stacked.append(jnp.where(low, qp, zero))
            stacked.append(jnp.where(low, zero, qp))
        qs = jnp.concatenate(stacked, axis=0)
        s = _dot_nt(qs, kb) + bias_ref[placement]
        m = jnp.maximum(jnp.max(s, axis=-1, keepdims=True), sink_col)
        p = jnp.exp2((s - m).astype(BF16))
        acc = _dot(p, vb)
        o = acc[:, :LANES] / (acc[:, LANES:] + jnp.exp2(sink_col - m))
        for pr in range(2):
            o_pair = jnp.where(low, o[(2 * pr) * BLOCK:(2 * pr + 1) * BLOCK],
                               o[(2 * pr + 1) * BLOCK:(2 * pr + 2) * BLOCK])
            o_ref[0, pl.ds(q0, BLOCK), pr * LANES:(pr + 1) * LANES] = o_pair.astype(o_ref.dtype)
        return carry

    lax.fori_loop(0, nb, block, 0, unroll=2)


def _swa_attention(slopes, sink, q, k, v):
    B, _, S, _ = q.shape
    smem = pl.BlockSpec(memory_space=pltpu.SMEM)
    return pl.pallas_call(
        _swa_attn_kernel,
        grid=(B, SWA_KV_HEADS),
        in_specs=[
            smem, smem,
            pl.BlockSpec((1, 2, S, LANES), lambda b, g: (b, g, 0, 0)),
            pl.BlockSpec((1, 1, S, LANES), lambda b, g: (b, g, 0, 0)),
            pl.BlockSpec((1, 1, S, 2 * LANES), lambda b, g: (b, g, 0, 0)),
        ],
        out_specs=pl.BlockSpec((1, S, 2 * LANES), lambda b, g: (b, 0, g)),
        out_shape=jax.ShapeDtypeStruct((B, S, SWA_HEADS * SWA_HD), BF16),
        scratch_shapes=[pltpu.VMEM((len(_SWA_SHIFTS), SWA_GROUP * BLOCK, SPAN), F32)],
        compiler_params=_params("parallel", "parallel"),
        name="swa_attention",
    )(slopes, sink, q, k, v)


def _post_kernel(oa_ref, ob_ref, x_ref, woa_ref, wob_ref, g1_ref, b1_ref,
                 w1_ref, fb1_ref, w2_ref, fb2_ref, g2_ref, b2_ref, out_ref):
    y = _dot(oa_ref[...], woa_ref[...]) + _dot(ob_ref[...], wob_ref[...])
    x1 = _layer_norm(ALPHA * x_ref[...] + y, g1_ref[...], b1_ref[...])
    x1b = x1.astype(BF16)
    acc = jnp.zeros_like(x1) + fb2_ref[...]
    for c in range(0, w1_ref.shape[1], FF_CHUNK):
        hc = _dot(x1b, w1_ref[:, c:c + FF_CHUNK]) + fb1_ref[:, c:c + FF_CHUNK]
        hc = jnp.square(jnp.maximum(hc, 0.0))
        acc = acc + _dot(hc.astype(BF16), w2_ref[c:c + FF_CHUNK, :])
    out_ref[...] = _layer_norm(ALPHA * x1 + acc, g2_ref[...], b2_ref[...])


def _post(o_a, o_b, a_blk, b_blk, x, wo_a, wo_b, g1, b1, w1, fb1, w2, fb2, g2, b2):
    T, D = x.shape
    tm = min(TOKEN_TILE, T)
    half = wo_a.shape[0]
    const = lambda i: (0, 0)
    resident = lambda a: pl.BlockSpec(a.shape, const, pipeline_mode=pl.Buffered(1))
    return pl.pallas_call(
        _post_kernel,
        grid=(T // tm,),
        in_specs=[
            pl.BlockSpec((tm, half), lambda i: (i, a_blk)),
            pl.BlockSpec((tm, half), lambda i: (i, b_blk)),
            pl.BlockSpec((tm, D), lambda i: (i, 0)),
            resident(wo_a), resident(wo_b), resident(g1), resident(b1),
            resident(w1), resident(fb1), resident(w2), resident(fb2), resident(g2), resident(b2),
        ],
        out_specs=pl.BlockSpec((tm, D), lambda i: (i, 0)),
        out_shape=jax.ShapeDtypeStruct((T, D), F32),
        compiler_params=_params("parallel"),
        name="post_mlp",
    )(o_a, o_b, x, wo_a, wo_b, g1, b1, w1, fb1, w2, fb2, g2, b2)


def _prep_even_weights(w_in, w_uq, w_ukv):
    D = w_in.shape[0]
    o_kr = MLA_Q_LORA + MLA_KV_LORA
    o_dq = o_kr + MLA_ROPE
    kr_slab = jnp.zeros((D, LANES), F32).at[:, MLA_NOPE:MLA_NOPE + MLA_ROPE].set(w_in[:, o_kr:o_dq])
    w_in_p = jnp.concatenate([w_in[:, :o_kr], kr_slab, w_in[:, o_dq:]], axis=1).astype(BF16)
    assert w_in_p.shape[1] == _EV_END

    r = w_uq.shape[0]
    qh = w_uq.reshape(r, MLA_HEADS, MLA_NOPE + MLA_ROPE)
    qh = jnp.pad(qh, ((0, 0), (0, 0), (0, LANES - MLA_NOPE - MLA_ROPE)))
    w_uq_p = qh.reshape(r, MLA_HEADS * LANES).astype(BF16)

    r = w_ukv.shape[0]
    kvh = w_ukv.reshape(r, MLA_HEADS, MLA_NOPE + MLA_V)
    kn = jnp.pad(kvh[:, :, :MLA_NOPE], ((0, 0), (0, 0), (0, LANES - MLA_NOPE)))
    vh = jnp.pad(kvh[:, :, MLA_NOPE:], ((0, 0), (0, 0), (0, LANES - MLA_V)))
    w_ukv_p = jnp.concatenate([kn.reshape(r, MLA_HEADS * LANES),
                               vh.reshape(r, MLA_HEADS * LANES)], axis=1).astype(BF16)
    return w_in_p, w_uq_p, w_ukv_p


def _prep_odd_weights(w_in):
    D = w_in.shape[0]
    nq = SWA_HEADS * SWA_HD
    nkv = SWA_KV_HEADS * SWA_HD
    dup = lambda w: jnp.concatenate([w.reshape(D, SWA_KV_HEADS, SWA_HD)] * 2, axis=2).reshape(D, 2 * nkv)
    return jnp.concatenate([w_in[:, :nq], dup(w_in[:, nq:nq + nkv]), dup(w_in[:, nq + nkv:])],
                           axis=1).astype(BF16)


def _rope_tables(S):
    half = MLA_ROPE // 2
    inv = ROPE_THETA ** (-np.arange(half, dtype=np.float32) / half)
    ang = np.arange(S, dtype=np.float32)[:, None] * inv[None, :]
    cos, sin = np.cos(ang).astype(np.float32), np.sin(ang).astype(np.float32)
    c = np.zeros((S, LANES), np.float32)
    sn = np.zeros((S, LANES), np.float32)
    sp = np.zeros((S, LANES), np.float32)
    c[:, :MLA_NOPE] = 1.0
    c[:, MLA_NOPE:MLA_NOPE + half] = cos
    c[:, MLA_NOPE + half:MLA_NOPE + 2 * half] = cos
    sn[:, MLA_NOPE:MLA_NOPE + half] = -sin
    sp[:, MLA_NOPE + half:MLA_NOPE + 2 * half] = sin
    return jnp.asarray(c), jnp.asarray(sn), jnp.asarray(sp)


def _alibi_slopes_log2(n):
    return jnp.asarray(2.0 ** (-8.0 * np.arange(1, n + 1, dtype=np.float32) / n) * LOG2E, F32)


def kernel(x, ev_w_in, ev_q_norm, ev_kv_norm, ev_w_uq, ev_w_ukv, ev_lam_q1, ev_lam_k1, ev_lam_q2, ev_lam_k2, ev_diff_norm, ev_w_out, od_w_in, od_sink, od_w_out, ln1_g, ln1_b, ln2_g, ln2_b, ffn_w1, ffn_b1, ffn_w2, ffn_b2):
    B, S, D = x.shape
    row = lambda a: a.reshape(1, -1).astype(F32)
    rope_c, rope_sn, rope_sp = _rope_tables(S)
    xs = x
    for layer in range(DEPTH):
        j = layer // 2
        if layer % 2 == 0:
            lambda_init = 0.8 - 0.6 * math.exp(-0.3 * layer)
            w_in_p, w_uq_p, w_ukv_p = _prep_even_weights(ev_w_in[j], ev_w_uq[j], ev_w_ukv[j])
            q, k, v, dq, dk, dv = _even_in_proj(xs, w_in_p, row(ev_q_norm[j]), row(ev_kv_norm[j]),
                                                w_uq_p, w_ukv_p, rope_c, rope_sn, rope_sp)
            o_mla = _mla_attention(q, k, v)
            lam_vecs = jnp.stack([ev_lam_q1[j], ev_lam_k1[j], ev_lam_q2[j], ev_lam_k2[j]]).astype(F32)
            o_diff = _diff_attention(_alibi_slopes_log2(DIFF_HEADS), lam_vecs, dq, dk, dv,
                                     row(ev_diff_norm[j]), lambda_init)
            n_a = o_mla.shape[-1]
            o_a, o_b, a_blk, b_blk = o_mla.reshape(B * S, n_a), o_diff.reshape(B * S, -1), 0, 0
            w_out = ev_w_out[j]
        else:
            q, k, v = _odd_in_proj(xs, _prep_odd_weights(od_w_in[j]))
            o = _swa_attention(_alibi_slopes_log2(SWA_HEADS), od_sink[j].astype(F32), q, k, v)
            n_a = o.shape[-1] // 2
            o_a = o_b = o.reshape(B * S, -1)
            a_blk, b_blk = 0, 1
            w_out = od_w_out[j]
        xs = _post(o_a, o_b, a_blk, b_blk, xs.reshape(B * S, D),
                   w_out[:n_a].astype(BF16), w_out[n_a:].astype(BF16),
                   row(ln1_g[layer]), row(ln1_b[layer]),
                   ffn_w1[layer].astype(BF16), row(ffn_b1[layer]),
                   ffn_w2[layer].astype(BF16), row(ffn_b2[layer]),
                   row(ln2_g[layer]), row(ln2_b[layer])).reshape(B, S, D)
    return xs
```

```python
import functools
import math

import jax
import jax.numpy as jnp
import numpy as np
from jax import lax
from jax.experimental import pallas as pl
from jax.experimental.pallas import tpu as pltpu

F32 = jnp.float32
BF16 = jnp.bfloat16

D_MODEL = 1024
DEPTH = 2
MLA_HEADS = 8
MLA_NOPE = 64
MLA_ROPE = 32
MLA_V = 64
MLA_Q_LORA = 384
MLA_KV_LORA = 256
ROPE_THETA = 10000.0
DIFF_HEADS = 4
DIFF_HD = 64
DIFF_VD = 2 * DIFF_HD
SWA_HEADS = 16
SWA_KV_HEADS = 4
SWA_HD = 64
SWA_GROUP = SWA_HEADS // SWA_KV_HEADS
WINDOW = 128
BLOCK = 128
D_FF = 4 * D_MODEL
ALPHA = (2 * DEPTH) ** 0.25
LN_EPS = 1e-5
RMS_EPS = 1e-6
LOG2E = math.log2(math.e)

LANES = 128
HALF = LANES // 2
SPAN = BLOCK + 2 * WINDOW

_EV_CQ = 0
_EV_CKV = _EV_CQ + MLA_Q_LORA
_EV_KR = _EV_CKV + MLA_KV_LORA
_EV_DQ = _EV_KR + LANES
_EV_DK = _EV_DQ + DIFF_HEADS * LANES
_EV_DV = _EV_DK + DIFF_HEADS * LANES
_EV_END = _EV_DV + DIFF_HEADS * LANES

VMEM_LIMIT_BYTES = 56 * 1024 * 1024

TOKEN_TILE = 512
QUERY_TILE = 512
KEY_CHUNK = 512
FF_CHUNK = 1024

BOUND_SLACK = 1.0 + 2.0 ** -6
MIN_ROW_SUM = 2.0 ** -100


def _params(*sem):
    return pltpu.CompilerParams(dimension_semantics=sem, vmem_limit_bytes=VMEM_LIMIT_BYTES)


def _dot(a, b):
    return jnp.dot(a, b, preferred_element_type=F32)


def _dot_nt(a, b):
    return lax.dot_general(a, b, (((1,), (1,)), ((), ())), preferred_element_type=F32)


def _rms(x, g):
    return x * lax.rsqrt(jnp.mean(x * x, axis=-1, keepdims=True) + RMS_EPS) * g


def _layer_norm(x, g, b):
    mu = jnp.mean(x, axis=-1, keepdims=True)
    xc = x - mu
    var = jnp.mean(xc * xc, axis=-1, keepdims=True)
    return xc * lax.rsqrt(var + LN_EPS) * g + b


def _rope_slab(blk, c, s_next, s_prev):
    nxt = pltpu.roll(blk, LANES - MLA_ROPE // 2, 1)
    prv = pltpu.roll(blk, MLA_ROPE // 2, 1)
    return blk * c + nxt * s_next + prv * s_prev


def _softmax_update(s, v, m, acc):
    m_new = jnp.max(s, axis=-1, keepdims=True)
    if m is not None:
        m_new = jnp.maximum(m, m_new)
    p = jnp.exp2((s - m_new).astype(BF16))
    pv = _dot(p, v)
    if m is not None:
        pv = acc * jnp.exp2(m - m_new) + pv
    return m_new, pv


def _sq_norm(x):
    xf = x.astype(F32)
    return jnp.sum(xf * xf, axis=1, keepdims=True)


def _max_sq_norm_tile(x):
    return jnp.broadcast_to(jnp.max(_sq_norm(x), axis=0, keepdims=True), (8, LANES))


def _score_bound(q, kmax_tile):
    return jnp.sqrt(_sq_norm(q) * kmax_tile[0:1, 0:1]) * BOUND_SLACK


def _all_rows_healthy(row_sums):
    return jnp.min(jnp.where(row_sums >= MIN_ROW_SUM, 1.0, 0.0)) > 0.5


def _even_in_kernel(x_ref, w_in_ref, qn_ref, kvn_ref, w_uq_ref, w_ukv_ref, c_ref, sn_ref, sp_ref,
                    q_ref, k_ref, v_ref, dq_ref, dk_ref, dv_ref):
    xb = x_ref[0].astype(BF16)
    h = _dot(xb, w_in_ref[...])
    c, sn, sp = c_ref[...], sn_ref[...], sp_ref[...]
    low = lax.broadcasted_iota(jnp.int32, (xb.shape[0], LANES), 1) < HALF

    cq = _rms(h[:, _EV_CQ:_EV_CKV], qn_ref[...])
    q = _dot(cq.astype(BF16), w_uq_ref[...])
    q_scale = (MLA_NOPE + MLA_ROPE) ** -0.5 * LOG2E
    for hd in range(MLA_HEADS):
        blk = q[:, hd * LANES:(hd + 1) * LANES]
        q_ref[0, hd] = (_rope_slab(blk, c, sn, sp) * q_scale).astype(BF16)

    ckv = _rms(h[:, _EV_CKV:_EV_KR], kvn_ref[...])
    kv = _dot(ckv.astype(BF16), w_ukv_ref[...])
    kr = _rope_slab(h[:, _EV_KR:_EV_DQ], c, sn, sp)
    v_off = MLA_HEADS * LANES
    for hd in range(MLA_HEADS):
        k_ref[0, hd] = (kv[:, hd * LANES:(hd + 1) * LANES] + kr).astype(BF16)
        vh = kv[:, v_off + hd * LANES:v_off + (hd + 1) * LANES]
        v_ref[0, hd] = jnp.where(low, vh, 1.0).astype(BF16)

    dq_scale = DIFF_HD ** -0.5 * LOG2E
    ones = jnp.ones((xb.shape[0], LANES), BF16)
    for hd in range(DIFF_HEADS):
        dq_ref[0, hd] = (h[:, _EV_DQ + hd * LANES:_EV_DQ + (hd + 1) * LANES] * dq_scale).astype(BF16)
        dk_ref[0, hd] = h[:, _EV_DK + hd * LANES:_EV_DK + (hd + 1) * LANES].astype(BF16)
        dv_ref[0, hd, :, :LANES] = h[:, _EV_DV + hd * LANES:_EV_DV + (hd + 1) * LANES].astype(BF16)
        dv_ref[0, hd, :, LANES:] = ones


def _even_in_proj(x, w_in_p, q_norm, kv_norm, w_uq_p, w_ukv_p, rope_c, rope_sn, rope_sp):
    B, S, D = x.shape
    tm = min(TOKEN_TILE, S)
    nt = S // tm
    const = lambda b, i: (0, 0)
    head_out = lambda n, w=LANES: pl.BlockSpec((1, n, tm, w), lambda b, i: (b, 0, i, 0))
    head_shape = lambda n, w=LANES: jax.ShapeDtypeStruct((B, n, S, w), BF16)
    return pl.pallas_call(
        _even_in_kernel,
        grid=(B, nt),
        in_specs=[
            pl.BlockSpec((1, tm, D), lambda b, i: (b, i, 0)),
            pl.BlockSpec(w_in_p.shape, const),
            pl.BlockSpec(q_norm.shape, const),
            pl.BlockSpec(kv_norm.shape, const),
            pl.BlockSpec(w_uq_p.shape, const),
            pl.BlockSpec(w_ukv_p.shape, const),
            pl.BlockSpec((tm, LANES), lambda b, i: (i, 0)),
            pl.BlockSpec((tm, LANES), lambda b, i: (i, 0)),
            pl.BlockSpec((tm, LANES), lambda b, i: (i, 0)),
        ],
        out_specs=[head_out(MLA_HEADS), head_out(MLA_HEADS), head_out(MLA_HEADS),
                   head_out(DIFF_HEADS), head_out(DIFF_HEADS), head_out(DIFF_HEADS, 2 * LANES)],
        out_shape=[head_shape(MLA_HEADS), head_shape(MLA_HEADS), head_shape(MLA_HEADS),
                   head_shape(DIFF_HEADS), head_shape(DIFF_HEADS), head_shape(DIFF_HEADS, 2 * LANES)],
        compiler_params=_params("parallel", "parallel"),
        name="even_in_proj",
    )(x, w_in_p, q_norm, kv_norm, w_uq_p, w_ukv_p, rope_c, rope_sn, rope_sp)


def _mla_attn_kernel(q_ref, k_ref, v_ref, o_ref, kmax_ref):
    @pl.when(pl.program_id(2) == 0)
    def _():
        for hh in range(2):
            kmax_ref[hh] = _max_sq_norm_tile(k_ref[0, hh])

    low = lax.broadcasted_iota(jnp.int32, (q_ref.shape[2], LANES), 1) < HALF

    def write(accs):
        sums0 = pltpu.roll(accs[0], HALF, 1)
        vals1 = pltpu.roll(accs[1], HALF, 1)
        o_ref[0] = jnp.where(low, accs[0] / sums0, vals1 / accs[1]).astype(o_ref.dtype)

    accs = []
    for hh in range(2):
        q = q_ref[0, hh]
        s = _dot_nt(q, k_ref[0, hh])
        p = jnp.exp2(s - _score_bound(q, kmax_ref[hh])).astype(BF16)
        accs.append(_dot(p, v_ref[0, hh]))
    write(accs)
    healthy = _all_rows_healthy(jnp.minimum(accs[0][:, HALF:], accs[1][:, HALF:]))

    @pl.when(jnp.logical_not(healthy))
    def _():
        exact = []
        for hh in range(2):
            s = _dot_nt(q_ref[0, hh], k_ref[0, hh])
            exact.append(_softmax_update(s, v_ref[0, hh], None, None)[1])
        write(exact)


def _mla_attention(q, k, v):
    B, H, S, _ = q.shape
    tq = min(QUERY_TILE, S)
    return pl.pallas_call(
        _mla_attn_kernel,
        grid=(B, H // 2, S // tq),
        in_specs=[
            pl.BlockSpec((1, 2, tq, LANES), lambda b, j, i: (b, j, i, 0)),
            pl.BlockSpec((1, 2, S, LANES), lambda b, j, i: (b, j, 0, 0)),
            pl.BlockSpec((1, 2, S, LANES), lambda b, j, i: (b, j, 0, 0)),
        ],
        out_specs=pl.BlockSpec((1, tq, LANES), lambda b, j, i: (b, i, j)),
        out_shape=jax.ShapeDtypeStruct((B, S, H * MLA_V), BF16),
        scratch_shapes=[pltpu.VMEM((2, 8, LANES), F32)],
        compiler_params=_params("parallel", "parallel", "arbitrary"),
        name="mla_attention",
    )(q, k, v)


def _diff_attn_kernel(slope_ref, lam_ref, q_ref, k_ref, v_ref, g_ref, o_ref, *, lambda_init):
    hd = pl.program_id(1)
    qi = pl.program_id(2)
    q = q_ref[0, 0]
    tq, S = q.shape[0], k_ref.shape[2]
    kc = min(KEY_CHUNK, S)

    lv = lam_ref[...]
    lam = (jnp.exp(jnp.sum(lv[0:1] * lv[1:2], axis=-1, keepdims=True))
           - jnp.exp(jnp.sum(lv[2:3] * lv[3:4], axis=-1, keepdims=True)) + lambda_init)

    slope = slope_ref[hd]
    q_pos = (qi * tq + lax.broadcasted_iota(jnp.int32, (tq, 1), 0)).astype(F32) * slope
    lane = lax.broadcasted_iota(jnp.int32, q.shape, 1)
    zero = jnp.zeros_like(q)
    q_maps = (jnp.where(lane < HALF, q, zero), jnp.where(lane >= HALF, q, zero))

    state = [(None, None), (None, None)]
    for c in range(0, S, kc):
        k = k_ref[0, 0, c:c + kc, :]
        v = v_ref[0, 0, c:c + kc, :]
        k_pos = (c + lax.broadcasted_iota(jnp.int32, (1, kc), 1)).astype(F32) * slope
        dist = jnp.abs(k_pos - q_pos)
        for mp in range(2):
            s = _dot_nt(q_maps[mp], k) - dist
            state[mp] = _softmax_update(s, v, *state[mp])
    outs = [acc[:, :LANES] / acc[:, LANES:] for _, acc in state]
    o = outs[0] - lam * outs[1]
    o = _rms(o, g_ref[...]) * (1.0 - lambda_init)
    o_ref[0] = o.astype(o_ref.dtype)


def _diff_attention(slopes, lam_vecs, q, k, v, diff_norm, lambda_init):
    B, H, S, _ = q.shape
    tq = min(QUERY_TILE, S)
    smem = pl.BlockSpec(memory_space=pltpu.SMEM)
    return pl.pallas_call(
        functools.partial(_diff_attn_kernel, lambda_init=lambda_init),
        grid=(B, H, S // tq),
        in_specs=[
            smem,
            pl.BlockSpec(lam_vecs.shape, lambda b, h, i: (0, 0)),
            pl.BlockSpec((1, 1, tq, LANES), lambda b, h, i: (b, h, i, 0)),
            pl.BlockSpec((1, 1, S, LANES), lambda b, h, i: (b, h, 0, 0)),
            pl.BlockSpec((1, 1, S, 2 * LANES), lambda b, h, i: (b, h, 0, 0)),
            pl.BlockSpec(diff_norm.shape, lambda b, h, i: (0, 0)),
        ],
        out_specs=pl.BlockSpec((1, tq, LANES), lambda b, h, i: (b, i, h)),
        out_shape=jax.ShapeDtypeStruct((B, S, H * DIFF_VD), BF16),
        compiler_params=_params("parallel", "parallel", "arbitrary"),
        name="diff_attention",
    )(slopes, lam_vecs, q, k, v, diff_norm)


def _odd_in_kernel(x_ref, w_ref, q_ref, k_ref, v_ref):
    h = _dot(x_ref[0].astype(BF16), w_ref[...])
    nq = SWA_HEADS // 2
    q_scale = SWA_HD ** -0.5 * LOG2E
    ones = jnp.ones((h.shape[0], LANES), BF16)
    for pr in range(nq):
        q_ref[0, pr] = (h[:, pr * LANES:(pr + 1) * LANES] * q_scale).astype(BF16)
    for g in range(SWA_KV_HEADS):
        k_ref[0, g] = h[:, (nq + g) * LANES:(nq + g + 1) * LANES].astype(BF16)
        v_lo = (nq + SWA_KV_HEADS + g) * LANES
        v_ref[0, g, :, :LANES] = h[:, v_lo:v_lo + LANES].astype(BF16)
        v_ref[0, g, :, LANES:] = ones


def _odd_in_proj(x, w_p):
    B, S, D = x.shape
    tm = min(TOKEN_TILE, S)
    head_out = lambda n, w=LANES: pl.BlockSpec((1, n, tm, w), lambda b, i: (b, 0, i, 0))
    head_shape = lambda n, w=LANES: jax.ShapeDtypeStruct((B, n, S, w), BF16)
    return pl.pallas_call(
        _odd_in_kernel,
        grid=(B, S // tm),
        in_specs=[
            pl.BlockSpec((1, tm, D), lambda b, i: (b, i, 0)),
            pl.BlockSpec(w_p.shape, lambda b, i: (0, 0)),
        ],
        out_specs=[head_out(SWA_HEADS // 2), head_out(SWA_KV_HEADS), head_out(SWA_KV_HEADS, 2 * LANES)],
        out_shape=[head_shape(SWA_HEADS // 2), head_shape(SWA_KV_HEADS),
                   head_shape(SWA_KV_HEADS, 2 * LANES)],
        compiler_params=_params("parallel", "parallel"),
        name="odd_in_proj",
    )(x, w_p)


_SWA_SHIFTS = (-WINDOW, 0, -2 * WINDOW)


def _swa_attn_kernel(slope_ref, sink_ref, q_ref, k_ref, v_ref, o_ref, bias_ref):
    g = pl.program_id(1)
    S = k_ref.shape[2]
    nb = S // BLOCK
    rows = SWA_GROUP * BLOCK
    low = lax.broadcasted_iota(jnp.int32, (BLOCK, LANES), 1) < HALF
    head_of_row = lax.broadcasted_iota(jnp.int32, (rows, 1), 0) // BLOCK
    slope_col = jnp.zeros((rows, 1), F32)
    sink_col = jnp.zeros((rows, 1), F32)
    for hh in range(SWA_GROUP):
        slope_col = jnp.where(head_of_row == hh, slope_ref[g * SWA_GROUP + hh], slope_col)
        sink_col = jnp.where(head_of_row == hh, sink_ref[g * SWA_GROUP + hh] * LOG2E, sink_col)

    rel0 = (lax.broadcasted_iota(jnp.int32, (rows, SPAN), 1)
            - lax.broadcasted_iota(jnp.int32, (rows, SPAN), 0) % BLOCK)
    for t, shift in enumerate(_SWA_SHIFTS):
        dist = jnp.abs(rel0 + shift)
        bias_ref[t] = jnp.where(dist <= WINDOW, -slope_col * dist.astype(F32), -jnp.inf)

    def block(i, carry):
        q0 = pl.multiple_of(i * BLOCK, BLOCK)
        ws = pl.multiple_of(jnp.clip(q0 - WINDOW, 0, S - SPAN), BLOCK)
        placement = jnp.where(i == 0, 1, jnp.where(i == nb - 1, 2, 0))
        kb = k_ref[0, 0, pl.ds(ws, SPAN), :]
        vb = v_ref[0, 0, pl.ds(ws, SPAN), :]
        stacked = []
        for pr in range(2):
            qp = q_ref[0, pr, pl.ds(q0, BLOCK), :]
            zero = jnp.zeros_like(qp)
            stacked.append(jnp.where(low, qp, zero))
            stacked.append(jnp.where(low, zero, qp))
        qs = jnp.concatenate(stacked, axis=0)
        s = _dot_nt(qs, kb) + bias_ref[placement]
        m = jnp.maximum(jnp.max(s, axis=-1, keepdims=True), sink_col)
        p = jnp.exp2((s - m).astype(BF16))
        acc = _dot(p, vb)
        o = acc[:, :LANES] / (acc[:, LANES:] + jnp.exp2(sink_col - m))
        for pr in range(2):
            o_pair = jnp.where(low, o[(2 * pr) * BLOCK:(2 * pr + 1) * BLOCK],
                               o[(2 * pr + 1) * BLOCK:(2 * pr + 2) * BLOCK])
            o_ref[0, pl.ds(q0, BLOCK), pr * LANES:(pr + 1) * LANES] = o_pair.astype(o_ref.dtype)
        return carry

    lax.fori_loop(0, nb, block, 0, unroll=2)


def _swa_attention(slopes, sink, q, k, v):
    B, _, S, _ = q.shape
    smem = pl.BlockSpec(memory_space=pltpu.SMEM)
    return pl.pallas_call(
        _swa_attn_kernel,
        grid=(B, SWA_KV_HEADS),
        in_specs=[
            smem, smem,
            pl.BlockSpec((1, 2, S, LANES), lambda b, g: (b, g, 0, 0)),
            pl.BlockSpec((1, 1, S, LANES), lambda b, g: (b, g, 0, 0)),
            pl.BlockSpec((1, 1, S, 2 * LANES), lambda b, g: (b, g, 0, 0)),
        ],
        out_specs=pl.BlockSpec((1, S, 2 * LANES), lambda b, g: (b, 0, g)),
        out_shape=jax.ShapeDtypeStruct((B, S, SWA_HEADS * SWA_HD), BF16),
        scratch_shapes=[pltpu.VMEM((len(_SWA_SHIFTS), SWA_GROUP * BLOCK, SPAN), F32)],
        compiler_params=_params("parallel", "parallel"),
        name="swa_attention",
    )(slopes, sink, q, k, v)


def _post_kernel(oa_ref, ob_ref, x_ref, woa_ref, wob_ref, g1_ref, b1_ref,
                 w1_ref, fb1_ref, w2_ref, fb2_ref, g2_ref, b2_ref, out_ref):
    y = _dot(oa_ref[...], woa_ref[...]) + _dot(ob_ref[...], wob_ref[...])
    x1 = _layer_norm(ALPHA * x_ref[...] + y, g1_ref[...], b1_ref[...])
    x1b = x1.astype(BF16)
    acc = jnp.zeros_like(x1) + fb2_ref[...]
    for c in range(0, w1_ref.shape[1], FF_CHUNK):
        hc = _dot(x1b, w1_ref[:, c:c + FF_CHUNK]) + fb1_ref[:, c:c + FF_CHUNK]
        hc = jnp.square(jnp.maximum(hc, 0.0))
        acc = acc + _dot(hc.astype(BF16), w2_ref[c:c + FF_CHUNK, :])
    out_ref[...] = _layer_norm(ALPHA * x1 + acc, g2_ref[...], b2_ref[...])


def _post(o_a, o_b, a_blk, b_blk, x, wo_a, wo_b, g1, b1, w1, fb1, w2, fb2, g2, b2):
    T, D = x.shape
    tm = min(TOKEN_TILE, T)
    half = wo_a.shape[0]
    const = lambda i: (0, 0)
    resident = lambda a: pl.BlockSpec(a.shape, const, pipeline_mode=pl.Buffered(1))
    return pl.pallas_call(
        _post_kernel,
        grid=(T // tm,),
        in_specs=[
            pl.BlockSpec((tm, half), lambda i: (i, a_blk)),
            pl.BlockSpec((tm, half), lambda i: (i, b_blk)),
            pl.BlockSpec((tm, D), lambda i: (i, 0)),
            resident(wo_a), resident(wo_b), resident(g1), resident(b1),
            resident(w1), resident(fb1), resident(w2), resident(fb2), resident(g2), resident(b2),
        ],
        out_specs=pl.BlockSpec((tm, D), lambda i: (i, 0)),
        out_shape=jax.ShapeDtypeStruct((T, D), F32),
        compiler_params=_params("parallel"),
        name="post_mlp",
    )(o_a, o_b, x, wo_a, wo_b, g1, b1, w1, fb1, w2, fb2, g2, b2)


def _prep_even_weights(w_in, w_uq, w_ukv):
    D = w_in.shape[0]
    o_kr = MLA_Q_LORA + MLA_KV_LORA
    o_dq = o_kr + MLA_ROPE
    kr_slab = jnp.zeros((D, LANES), F32).at[:, MLA_NOPE:MLA_NOPE + MLA_ROPE].set(w_in[:, o_kr:o_dq])
    w_in_p = jnp.concatenate([w_in[:, :o_kr], kr_slab, w_in[:, o_dq:]], axis=1).astype(BF16)
    assert w_in_p.shape[1] == _EV_END

    r = w_uq.shape[0]
    qh = w_uq.reshape(r, MLA_HEADS, MLA_NOPE + MLA_ROPE)
    qh = jnp.pad(qh, ((0, 0), (0, 0), (0, LANES - MLA_NOPE - MLA_ROPE)))
    w_uq_p = qh.reshape(r, MLA_HEADS * LANES).astype(BF16)

    r = w_ukv.shape[0]
    kvh = w_ukv.reshape(r, MLA_HEADS, MLA_NOPE + MLA_V)
    kn = jnp.pad(kvh[:, :, :MLA_NOPE], ((0, 0), (0, 0), (0, LANES - MLA_NOPE)))
    vh = jnp.pad(kvh[:, :, MLA_NOPE:], ((0, 0), (0, 0), (0, LANES - MLA_V)))
    w_ukv_p = jnp.concatenate([kn.reshape(r, MLA_HEADS * LANES),
                               vh.reshape(r, MLA_HEADS * LANES)], axis=1).astype(BF16)
    return w_in_p, w_uq_p, w_ukv_p


def _prep_odd_weights(w_in):
    D = w_in.shape[0]
    nq = SWA_HEADS * SWA_HD
    nkv = SWA_KV_HEADS * SWA_HD
    dup = lambda w: jnp.concatenate([w.reshape(D, SWA_KV_HEADS, SWA_HD)] * 2, axis=2).reshape(D, 2 * nkv)
    return jnp.concatenate([w_in[:, :nq], dup(w_in[:, nq:nq + nkv]), dup(w_in[:, nq + nkv:])],
                           axis=1).astype(BF16)


def _rope_tables(S):
    half = MLA_ROPE // 2
    inv = ROPE_THETA ** (-np.arange(half, dtype=np.float32) / half)
    ang = np.arange(S, dtype=np.float32)[:, None] * inv[None, :]
    cos, sin = np.cos(ang).astype(np.float32), np.sin(ang).astype(np.float32)
    c = np.zeros((S, LANES), np.float32)
    sn = np.zeros((S, LANES), np.float32)
    sp = np.zeros((S, LANES), np.float32)
    c[:, :MLA_NOPE] = 1.0
    c[:, MLA_NOPE:MLA_NOPE + half] = cos
    c[:, MLA_NOPE + half:MLA_NOPE + 2 * half] = cos
    sn[:, MLA_NOPE:MLA_NOPE + half] = -sin
    sp[:, MLA_NOPE + half:MLA_NOPE + 2 * half] = sin
    return jnp.asarray(c), jnp.asarray(sn), jnp.asarray(sp)


def _alibi_slopes_log2(n):
    return jnp.asarray(2.0 ** (-8.0 * np.arange(1, n + 1, dtype=np.float32) / n) * LOG2E, F32)


def kernel(x, ev_w_in, ev_q_norm, ev_kv_norm, ev_w_uq, ev_w_ukv, ev_lam_q1, ev_lam_k1, ev_lam_q2, ev_lam_k2, ev_diff_norm, ev_w_out, od_w_in, od_sink, od_w_out, ln1_g, ln1_b, ln2_g, ln2_b, ffn_w1, ffn_b1, ffn_w2, ffn_b2):
    B, S, D = x.shape
    row = lambda a: a.reshape(1, -1).astype(F32)
    rope_c, rope_sn, rope_sp = _rope_tables(S)
    xs = x
    for layer in range(DEPTH):
        j = layer // 2
        if layer % 2 == 0:
            lambda_init = 0.8 - 0.6 * math.exp(-0.3 * layer)
            w_in_p, w_uq_p, w_ukv_p = _prep_even_weights(ev_w_in[j], ev_w_uq[j], ev_w_ukv[j])
            q, k, v, dq, dk, dv = _even_in_proj(xs, w_in_p, row(ev_q_norm[j]), row(ev_kv_norm[j]),
                                                w_uq_p, w_ukv_p, rope_c, rope_sn, rope_sp)
            o_mla = _mla_attention(q, k, v)
            lam_vecs = jnp.stack([ev_lam_q1[j], ev_lam_k1[j], ev_lam_q2[j], ev_lam_k2[j]]).astype(F32)
            o_diff = _diff_attention(_alibi_slopes_log2(DIFF_HEADS), lam_vecs, dq, dk, dv,
                                     row(ev_diff_norm[j]), lambda_init)
            n_a = o_mla.shape[-1]
            o_a, o_b, a_blk, b_blk = o_mla.reshape(B * S, n_a), o_diff.reshape(B * S, -1), 0, 0
            w_out = ev_w_out[j]
        else:
            q, k, v = _odd_in_proj(xs, _prep_odd_weights(od_w_in[j]))
            o = _swa_attention(_alibi_slopes_log2(SWA_HEADS), od_sink[j].astype(F32), q, k, v)
            n_a = o.shape[-1] // 2
            o_a = o_b = o.reshape(B * S, -1)
            a_blk, b_blk = 0, 1
            w_out = od_w_out[j]
        xs = _post(o_a, o_b, a_blk, b_blk, xs.reshape(B * S, D),
                   w_out[:n_a].astype(BF16), w_out[n_a:].astype(BF16),
                   row(ln1_g[layer]), row(ln1_b[layer]),
                   ffn_w1[layer].astype(BF16), row(ffn_b1[layer]),
                   ffn_w2[layer].astype(BF16), row(ffn_b2[layer]),
                   row(ln2_g[layer]), row(ln2_b[layer])).reshape(B, S, D)
    return xs
```

```python
import functools
import math

import jax
import jax.numpy as jnp
import numpy as np
from jax import lax
from jax.experimental import pallas as pl
from jax.experimental.pallas import tpu as pltpu

F32 = jnp.float32
BF16 = jnp.bfloat16

D_MODEL = 1024
DEPTH = 2
MLA_HEADS = 8
MLA_NOPE = 64
MLA_ROPE = 32
MLA_V = 64
MLA_Q_LORA = 384
MLA_KV_LORA = 256
ROPE_THETA = 10000.0
DIFF_HEADS = 4
DIFF_HD = 64
DIFF_VD = 2 * DIFF_HD
SWA_HEADS = 16
SWA_KV_HEADS = 4
SWA_HD = 64
SWA_GROUP = SWA_HEADS // SWA_KV_HEADS
WINDOW = 128
BLOCK = 128
D_FF = 4 * D_MODEL
ALPHA = (2 * DEPTH) ** 0.25
LN_EPS = 1e-5
RMS_EPS = 1e-6
LOG2E = math.log2(math.e)

LANES = 128
HALF = LANES // 2
SPAN = BLOCK + 2 * WINDOW

_EV_CQ = 0
_EV_CKV = _EV_CQ + MLA_Q_LORA
_EV_KR = _EV_CKV + MLA_KV_LORA
_EV_DQ = _EV_KR + LANES
_EV_DK = _EV_DQ + DIFF_HEADS * LANES
_EV_DV = _EV_DK + DIFF_HEADS * LANES
_EV_END = _EV_DV + DIFF_HEADS * LANES

VMEM_LIMIT_BYTES = 56 * 1024 * 1024

TOKEN_TILE = 512
QUERY_TILE = 512
KEY_CHUNK = 512
FULL_QUERY_TILE = 1024
ATTN_UNIT = 512
ATTN_KEY_CHUNK = 256
ONES_ROWS = 16
FF_CHUNK = 1024

BOUND_SLACK = 1.0 + 2.0 ** -6
MIN_ROW_SUM = 2.0 ** -100


def _params(*sem):
    return pltpu.CompilerParams(dimension_semantics=sem, vmem_limit_bytes=VMEM_LIMIT_BYTES)


def _dot(a, b):
    return jnp.dot(a, b, preferred_element_type=F32)


def _dot_nt(a, b):
    return lax.dot_general(a, b, (((1,), (1,)), ((), ())), preferred_element_type=F32)


def _rms(x, g):
    return x * lax.rsqrt(jnp.mean(x * x, axis=-1, keepdims=True) + RMS_EPS) * g


def _layer_norm(x, g, b):
    mu = jnp.mean(x, axis=-1, keepdims=True)
    xc = x - mu
    var = jnp.mean(xc * xc, axis=-1, keepdims=True)
    return xc * lax.rsqrt(var + LN_EPS) * g + b


def _rope_slab(blk, c, s_next, s_prev):
    nxt = pltpu.roll(blk, LANES - MLA_ROPE // 2, 1)
    prv = pltpu.roll(blk, MLA_ROPE // 2, 1)
    return blk * c + nxt * s_next + prv * s_prev


def _softmax_update(s, v, m, acc):
    m_new = jnp.max(s, axis=-1, keepdims=True)
    if m is not None:
        m_new = jnp.maximum(m, m_new)
    p = jnp.exp2((s - m_new).astype(BF16))
    pv = _dot(p, v)
    if m is not None:
        pv = acc * jnp.exp2(m - m_new) + pv
    return m_new, pv


def _sq_norm(x):
    xf = x.astype(F32)
    return jnp.sum(xf * xf, axis=1, keepdims=True)


def _max_sq_norm_tile(x):
    return jnp.broadcast_to(jnp.max(_sq_norm(x), axis=0, keepdims=True), (8, LANES))


def _score_bound(q, kmax_tile):
    return jnp.sqrt(_sq_norm(q) * kmax_tile[0:1, 0:1]) * BOUND_SLACK


def _row_sq_norms(x):
    xf = x.astype(F32)
    sq = xf * xf
    hi = sq.astype(BF16)
    lo = (sq - hi.astype(F32)).astype(BF16)
    ones = jnp.ones((8, x.shape[1]), BF16)
    return (_dot_nt(ones, hi) + _dot_nt(ones, lo))[0:1]


def _pipelined_units(n_units, n_chunks, qk_chunk, pv_chunk):
    results = []
    cur = None
    for stage in range(n_units + 1):
        nxt = [] if stage < n_units else None
        acc = None
        for c in range(n_chunks):
            if nxt is not None:
                nxt.append(qk_chunk(stage, c))
            if cur is not None:
                d = pv_chunk(stage - 1, c, cur[c])
                acc = d if acc is None else acc + d
        if cur is not None:
            results.append(acc)
        cur = nxt
    return results


def _all_rows_healthy(row_sums):
    return jnp.min(jnp.where(row_sums >= MIN_ROW_SUM, 1.0, 0.0)) > 0.5


def _even_in_kernel(x_ref, w_in_ref, qn_ref, kvn_ref, w_uq_ref, w_ukv_ref, c_ref, sn_ref, sp_ref,
                    q_ref, k_ref, v_ref, dq_ref, dk_ref, dv_ref):
    xb = x_ref[0].astype(BF16)
    h = _dot(xb, w_in_ref[...])
    c, sn, sp = c_ref[...], sn_ref[...], sp_ref[...]
    low = lax.broadcasted_iota(jnp.int32, (xb.shape[0], LANES), 1) < HALF

    cq = _rms(h[:, _EV_CQ:_EV_CKV], qn_ref[...])
    q = _dot(cq.astype(BF16), w_uq_ref[...])
    q_scale = (MLA_NOPE + MLA_ROPE) ** -0.5 * LOG2E
    for hd in range(MLA_HEADS):
        blk = q[:, hd * LANES:(hd + 1) * LANES]
        q_ref[0, hd] = (_rope_slab(blk, c, sn, sp) * q_scale).astype(BF16)

    ckv = _rms(h[:, _EV_CKV:_EV_KR], kvn_ref[...])
    kv = _dot(ckv.astype(BF16), w_ukv_ref[...])
    kr = _rope_slab(h[:, _EV_KR:_EV_DQ], c, sn, sp)
    v_off = MLA_HEADS * LANES
    for hd in range(MLA_HEADS):
        k_ref[0, hd] = (kv[:, hd * LANES:(hd + 1) * LANES] + kr).astype(BF16)
        vh = kv[:, v_off + hd * LANES:v_off + (hd + 1) * LANES]
        v_ref[0, hd] = jnp.where(low, vh, 1.0).T.astype(BF16)

    dq_scale = DIFF_HD ** -0.5 * LOG2E
    ones = jnp.ones((ONES_ROWS, xb.shape[0]), BF16)
    for hd in range(DIFF_HEADS):
        dq_ref[0, hd] = (h[:, _EV_DQ + hd * LANES:_EV_DQ + (hd + 1) * LANES] * dq_scale).astype(BF16)
        dk_ref[0, hd] = h[:, _EV_DK + hd * LANES:_EV_DK + (hd + 1) * LANES].astype(BF16)
        dv_ref[0, hd, :LANES, :] = h[:, _EV_DV + hd * LANES:_EV_DV + (hd + 1) * LANES].T.astype(BF16)
        dv_ref[0, hd, LANES:, :] = ones


def _even_in_proj(x, w_in_p, q_norm, kv_norm, w_uq_p, w_ukv_p, rope_c, rope_sn, rope_sp):
    B, S, D = x.shape
    tm = min(TOKEN_TILE, S)
    nt = S // tm
    const = lambda b, i: (0, 0)
    head_out = lambda n, w=LANES: pl.BlockSpec((1, n, tm, w), lambda b, i: (b, 0, i, 0))
    head_shape = lambda n, w=LANES: jax.ShapeDtypeStruct((B, n, S, w), BF16)
    head_out_t = lambda n, r: pl.BlockSpec((1, n, r, tm), lambda b, i: (b, 0, 0, i))
    head_shape_t = lambda n, r: jax.ShapeDtypeStruct((B, n, r, S), BF16)
    return pl.pallas_call(
        _even_in_kernel,
        grid=(B, nt),
        in_specs=[
            pl.BlockSpec((1, tm, D), lambda b, i: (b, i, 0)),
            pl.BlockSpec(w_in_p.shape, const),
            pl.BlockSpec(q_norm.shape, const),
            pl.BlockSpec(kv_norm.shape, const),
            pl.BlockSpec(w_uq_p.shape, const),
            pl.BlockSpec(w_ukv_p.shape, const),
            pl.BlockSpec((tm, LANES), lambda b, i: (i, 0)),
            pl.BlockSpec((tm, LANES), lambda b, i: (i, 0)),
            pl.BlockSpec((tm, LANES), lambda b, i: (i, 0)),
        ],
        out_specs=[head_out(MLA_HEADS), head_out(MLA_HEADS), head_out_t(MLA_HEADS, LANES),
                   head_out(DIFF_HEADS), head_out(DIFF_HEADS), head_out_t(DIFF_HEADS, LANES + ONES_ROWS)],
        out_shape=[head_shape(MLA_HEADS), head_shape(MLA_HEADS), head_shape_t(MLA_HEADS, LANES),
                   head_shape(DIFF_HEADS), head_shape(DIFF_HEADS),
                   head_shape_t(DIFF_HEADS, LANES + ONES_ROWS)],
        compiler_params=_params("parallel", "parallel"),
        name="even_in_proj",
    )(x, w_in_p, q_norm, kv_norm, w_uq_p, w_ukv_p, rope_c, rope_sn, rope_sp)


def _mla_attn_kernel(q_ref, k_ref, vt_ref, o_ref, kmax_ref):
    S, tq = k_ref.shape[2], q_ref.shape[2]
    unit, kc = min(ATTN_UNIT, tq), min(ATTN_KEY_CHUNK, S)

    @pl.when(pl.program_id(2) == 0)
    def _():
        for hh in range(2):
            kmax_ref[hh] = _max_sq_norm_tile(k_ref[0, hh])

    units = [(hh, u) for u in range(tq // unit) for hh in range(2)]

    def q_of(i):
        hh, u = units[i]
        return q_ref[0, hh, u * unit:(u + 1) * unit, :]

    bounds = [jnp.sqrt(_row_sq_norms(q_of(i)) * kmax_ref[units[i][0]][0:1, 0:1]) * BOUND_SLACK
              for i in range(len(units))]

    def qk_chunk(i, c):
        return _dot_nt(k_ref[0, units[i][0], c * kc:(c + 1) * kc, :], q_of(i))

    def pv_chunk(i, c, st):
        pt = jnp.exp2(st - bounds[i]).astype(BF16)
        return _dot(vt_ref[0, units[i][0], :, c * kc:(c + 1) * kc], pt)

    def write(accs):
        for u in range(tq // unit):
            ot = jnp.concatenate([a[:HALF] / a[HALF:] for a in accs[2 * u:2 * u + 2]], axis=0)
            o_ref[0, u * unit:(u + 1) * unit, :] = ot.T.astype(o_ref.dtype)

    accs = _pipelined_units(len(units), S // kc, qk_chunk, pv_chunk)
    write(accs)
    sums = accs[0][HALF:HALF + 8]
    for a in accs[1:]:
        sums = jnp.minimum(sums, a[HALF:HALF + 8])
    healthy = _all_rows_healthy(sums)

    @pl.when(jnp.logical_not(healthy))
    def _():
        exact = []
        for i, (hh, _) in enumerate(units):
            st = _dot_nt(k_ref[0, hh], q_of(i))
            pt = jnp.exp2(st - jnp.max(st, axis=0, keepdims=True)).astype(BF16)
            exact.append(_dot(vt_ref[0, hh], pt))
        write(exact)


def _mla_attention(q, k, v):
    B, H, S, _ = q.shape
    tq = min(FULL_QUERY_TILE, S)
    return pl.pallas_call(
        _mla_attn_kernel,
        grid=(B, H // 2, S // tq),
        in_specs=[
            pl.BlockSpec((1, 2, tq, LANES), lambda b, j, i: (b, j, i, 0)),
            pl.BlockSpec((1, 2, S, LANES), lambda b, j, i: (b, j, 0, 0)),
            pl.BlockSpec((1, 2, LANES, S), lambda b, j, i: (b, j, 0, 0)),
        ],
        out_specs=pl.BlockSpec((1, tq, LANES), lambda b, j, i: (b, i, j)),
        out_shape=jax.ShapeDtypeStruct((B, S, H * MLA_V), BF16),
        scratch_shapes=[pltpu.VMEM((2, 8, LANES), F32)],
        compiler_params=_params("parallel", "parallel", "arbitrary"),
        name="mla_attention",
    )(q, k, v)


def _diff_attn_kernel(slope_ref, lam_ref, q_ref, k_ref, vt_ref, g_ref, o_ref, kmax_ref, *, lambda_init):
    hd = pl.program_id(1)
    qi = pl.program_id(2)
    S, tq = k_ref.shape[2], q_ref.shape[2]
    unit, kc = min(ATTN_UNIT, tq), min(ATTN_KEY_CHUNK, S)

    @pl.when(qi == 0)
    def _():
        k = k_ref[0, 0]
        k_lane = lax.broadcasted_iota(jnp.int32, k.shape, 1)
        k_zero = jnp.zeros_like(k)
        kmax_ref[0] = _max_sq_norm_tile(jnp.where(k_lane < HALF, k, k_zero))
        kmax_ref[1] = _max_sq_norm_tile(jnp.where(k_lane >= HALF, k, k_zero))

    lv = lam_ref[...]
    lam = (jnp.exp(jnp.sum(lv[0:1] * lv[1:2], axis=-1, keepdims=True))
           - jnp.exp(jnp.sum(lv[2:3] * lv[3:4], axis=-1, keepdims=True)) + lambda_init)
    slope = slope_ref[hd]

    units = [(mp, u) for u in range(tq // unit) for mp in range(2)]
    lane = lax.broadcasted_iota(jnp.int32, (unit, LANES), 1)

    def q_of(i):
        mp, u = units[i]
        q = q_ref[0, 0, u * unit:(u + 1) * unit, :]
        return jnp.where((lane < HALF) if mp == 0 else (lane >= HALF), q, jnp.zeros_like(q))

    def q_pos(i):
        u = units[i][1]
        return (qi * tq + u * unit + lax.broadcasted_iota(jnp.int32, (1, unit), 1)).astype(F32) * slope

    def k_pos(c, n):
        return (c + lax.broadcasted_iota(jnp.int32, (n, 1), 0)).astype(F32) * slope

    bounds = [jnp.sqrt(_row_sq_norms(q_of(i)) * kmax_ref[units[i][0]][0:1, 0:1]) * BOUND_SLACK
              for i in range(len(units))]

    def qk_chunk(i, c):
        return _dot_nt(k_ref[0, 0, c * kc:(c + 1) * kc, :], q_of(i))

    def pv_chunk(i, c, st):
        dist = jnp.abs(k_pos(c * kc, kc) - q_pos(i))
        pt = jnp.exp2(st - dist - bounds[i]).astype(BF16)
        return _dot(vt_ref[0, 0, :, c * kc:(c + 1) * kc], pt)

    def write(accs):
        for u in range(tq // unit):
            outs = [a[:LANES] / a[LANES:LANES + 1] for a in accs[2 * u:2 * u + 2]]
            o = (outs[0] - lam * outs[1]).T
            o = _rms(o, g_ref[...]) * (1.0 - lambda_init)
            o_ref[0, u * unit:(u + 1) * unit, :] = o.astype(o_ref.dtype)

    accs = _pipelined_units(len(units), S // kc, qk_chunk, pv_chunk)
    write(accs)
    sums = accs[0][LANES:LANES + 8]
    for a in accs[1:]:
        sums = jnp.minimum(sums, a[LANES:LANES + 8])
    healthy = _all_rows_healthy(sums)

    @pl.when(jnp.logical_not(healthy))
    def _():
        exact = []
        for i in range(len(units)):
            st = _dot_nt(k_ref[0, 0], q_of(i)) - jnp.abs(k_pos(0, S) - q_pos(i))
            pt = jnp.exp2(st - jnp.max(st, axis=0, keepdims=True)).astype(BF16)
            exact.append(_dot(vt_ref[0, 0], pt))
        write(exact)


def _diff_attention(slopes, lam_vecs, q, k, v, diff_norm, lambda_init):
    B, H, S, _ = q.shape
    tq = min(FULL_QUERY_TILE, S)
    smem = pl.BlockSpec(memory_space=pltpu.SMEM)
    return pl.pallas_call(
        functools.partial(_diff_attn_kernel, lambda_init=lambda_init),
        grid=(B, H, S // tq),
        in_specs=[
            smem,
            pl.BlockSpec(lam_vecs.shape, lambda b, h, i: (0, 0)),
            pl.BlockSpec((1, 1, tq, LANES), lambda b, h, i: (b, h, i, 0)),
            pl.BlockSpec((1, 1, S, LANES), lambda b, h, i: (b, h, 0, 0)),
            pl.BlockSpec((1, 1, LANES + ONES_ROWS, S), lambda b, h, i: (b, h, 0, 0)),
            pl.BlockSpec(diff_norm.shape, lambda b, h, i: (0, 0)),
        ],
        out_specs=pl.BlockSpec((1, tq, LANES), lambda b, h, i: (b, i, h)),
        out_shape=jax.ShapeDtypeStruct((B, S, H * DIFF_VD), BF16),
        scratch_shapes=[pltpu.VMEM((2, 8, LANES), F32)],
        compiler_params=_params("parallel", "parallel", "arbitrary"),
        name="diff_attention",
    )(slopes, lam_vecs, q, k, v, diff_norm)


def _odd_in_kernel(x_ref, w_ref, q_ref, k_ref, v_ref):
    h = _dot(x_ref[0].astype(BF16), w_ref[...])
    nq = SWA_HEADS // 2
    q_scale = SWA_HD ** -0.5 * LOG2E
    ones = jnp.ones((h.shape[0], LANES), BF16)
    for pr in range(nq):
        q_ref[0, pr] = (h[:, pr * LANES:(pr + 1) * LANES] * q_scale).astype(BF16)
    for g in range(SWA_KV_HEADS):
        k_ref[0, g] = h[:, (nq + g) * LANES:(nq + g + 1) * LANES].astype(BF16)
        v_lo = (nq + SWA_KV_HEADS + g) * LANES
        v_ref[0, g, :, :LANES] = h[:, v_lo:v_lo + LANES].astype(BF16)
        v_ref[0, g, :, LANES:] = ones


def _odd_in_proj(x, w_p):
    B, S, D = x.shape
    tm = min(TOKEN_TILE, S)
    head_out = lambda n, w=LANES: pl.BlockSpec((1, n, tm, w), lambda b, i: (b, 0, i, 0))
    head_shape = lambda n, w=LANES: jax.ShapeDtypeStruct((B, n, S, w), BF16)
    return pl.pallas_call(
        _odd_in_kernel,
        grid=(B, S // tm),
        in_specs=[
            pl.BlockSpec((1, tm, D), lambda b, i: (b, i, 0)),
            pl.BlockSpec(w_p.shape, lambda b, i: (0, 0)),
        ],
        out_specs=[head_out(SWA_HEADS // 2), head_out(SWA_KV_HEADS), head_out(SWA_KV_HEADS, 2 * LANES)],
        out_shape=[head_shape(SWA_HEADS // 2), head_shape(SWA_KV_HEADS),
                   head_shape(SWA_KV_HEADS, 2 * LANES)],
        compiler_params=_params("parallel", "parallel"),
        name="odd_in_proj",
    )(x, w_p)


_SWA_SHIFTS = (-WINDOW, 0, -2 * WINDOW)


def _swa_attn_kernel(slope_ref, sink_ref, q_ref, k_ref, v_ref, o_ref, bias_ref):
    g = pl.program_id(1)
    S = k_ref.shape[2]
    nb = S // BLOCK
    rows = SWA_GROUP * BLOCK
    low = lax.broadcasted_iota(jnp.int32, (BLOCK, LANES), 1) < HALF
    head_of_row = lax.broadcasted_iota(jnp.int32, (rows, 1), 0) // BLOCK
    slope_col = jnp.zeros((rows, 1), F32)
    sink_col = jnp.zeros((rows, 1), F32)
    for hh in range(SWA_GROUP):
        slope_col = jnp.where(head_of_row == hh, slope_ref[g * SWA_GROUP + hh], slope_col)
        sink_col = jnp.where(head_of_row == hh, sink_ref[g * SWA_GROUP + hh] * LOG2E, sink_col)

    rel0 = (lax.broadcasted_iota(jnp.int32, (rows, SPAN), 1)
            - lax.broadcasted_iota(jnp.int32, (rows, SPAN), 0) % BLOCK)
    for t, shift in enumerate(_SWA_SHIFTS):
        dist = jnp.abs(rel0 + shift)
        bias_ref[t] = jnp.where(dist <= WINDOW, -slope_col * dist.astype(F32), -jnp.inf)

    def block(i, carry):
        q0 = pl.multiple_of(i * BLOCK, BLOCK)
        ws = pl.multiple_of(jnp.clip(q0 - WINDOW, 0, S - SPAN), BLOCK)
        placement = jnp.where(i == 0, 1, jnp.where(i == nb - 1, 2, 0))
        kb = k_ref[0, 0, pl.ds(ws, SPAN), :]
        vb = v_ref[0, 0, pl.ds(ws, SPAN), :]
        stacked = []
        for pr in range(2):
            qp = q_ref[0, pr, pl.ds(q0, BLOCK), :]
            zero = jnp.zeros_like(qp)
            stacked.append(jnp.where(low, qp, zero))
            stacked.append(jnp.where(low, zero, qp))
        qs = jnp.concatenate(stacked, axis=0)
        s = _dot_nt(qs, kb) + bias_ref[placement]
        m = jnp.maximum(jnp.max(s, axis=-1, keepdims=True), sink_col)
        p = jnp.exp2((s - m).astype(BF16))
        acc = _dot(p, vb)
        o = acc[:, :LANES] / (acc[:, LANES:] + jnp.exp2(sink_col - m))
        for pr in range(2):
            o_pair = jnp.where(low, o[(2 * pr) * BLOCK:(2 * pr + 1) * BLOCK],
                               o[(2 * pr + 1) * BLOCK:(2 * pr + 2) * BLOCK])
            o_ref[0, pl.ds(q0, BLOCK), pr * LANES:(pr + 1) * LANES] = o_pair.astype(o_ref.dtype)
        return carry

    lax.fori_loop(0, nb, block, 0, unroll=2)


def _swa_attention(slopes, sink, q, k, v):
    B, _, S, _ = q.shape
    smem = pl.BlockSpec(memory_space=pltpu.SMEM)
    return pl.pallas_call(
        _swa_attn_kernel,
        grid=(B, SWA_KV_HEADS),
        in_specs=[
            smem, smem,
            pl.BlockSpec((1, 2, S, LANES), lambda b, g: (b, g, 0, 0)),
            pl.BlockSpec((1, 1, S, LANES), lambda b, g: (b, g, 0, 0)),
            pl.BlockSpec((1, 1, S, 2 * LANES), lambda b, g: (b, g, 0, 0)),
        ],
        out_specs=pl.BlockSpec((1, S, 2 * LANES), lambda b, g: (b, 0, g)),
        out_shape=jax.ShapeDtypeStruct((B, S, SWA_HEADS * SWA_HD), BF16),
        scratch_shapes=[pltpu.VMEM((len(_SWA_SHIFTS), SWA_GROUP * BLOCK, SPAN), F32)],
        compiler_params=_params("parallel", "parallel"),
        name="swa_attention",
    )(slopes, sink, q, k, v)


def _post_kernel(oa_ref, ob_ref, x_ref, woa_ref, wob_ref, g1_ref, b1_ref,
                 w1_ref, fb1_ref, w2_ref, fb2_ref, g2_ref, b2_ref, out_ref):
    y = _dot(oa_ref[...], woa_ref[...]) + _dot(ob_ref[...], wob_ref[...])
    x1 = _layer_norm(ALPHA * x_ref[...] + y, g1_ref[...], b1_ref[...])
    x1b = x1.astype(BF16)
    acc = jnp.zeros_like(x1) + fb2_ref[...]
    for c in range(0, w1_ref.shape[1], FF_CHUNK):
        hc = _dot(x1b, w1_ref[:, c:c + FF_CHUNK]) + fb1_ref[:, c:c + FF_CHUNK]
        hc = jnp.square(jnp.maximum(hc, 0.0))
        acc = acc + _dot(hc.astype(BF16), w2_ref[c:c + FF_CHUNK, :])
    out_ref[...] = _layer_norm(ALPHA * x1 + acc, g2_ref[...], b2_ref[...])


def _post(o_a, o_b, a_blk, b_blk, x, wo_a, wo_b, g1, b1, w1, fb1, w2, fb2, g2, b2):
    T, D = x.shape
    tm = min(TOKEN_TILE, T)
    half = wo_a.shape[0]
    const = lambda i: (0, 0)
    resident = lambda a: pl.BlockSpec(a.shape, const, pipeline_mode=pl.Buffered(1))
    return pl.pallas_call(
        _post_kernel,
        grid=(T // tm,),
        in_specs=[
            pl.BlockSpec((tm, half), lambda i: (i, a_blk)),
            pl.BlockSpec((tm, half), lambda i: (i, b_blk)),
            pl.BlockSpec((tm, D), lambda i: (i, 0)),
            resident(wo_a), resident(wo_b), resident(g1), resident(b1),
            resident(w1), resident(fb1), resident(w2), resident(fb2), resident(g2), resident(b2),
        ],
        out_specs=pl.BlockSpec((tm, D), lambda i: (i, 0)),
        out_shape=jax.ShapeDtypeStruct((T, D), F32),
        compiler_params=_params("parallel"),
        name="post_mlp",
    )(o_a, o_b, x, wo_a, wo_b, g1, b1, w1, fb1, w2, fb2, g2, b2)


def _prep_even_weights(w_in, w_uq, w_ukv):
    D = w_in.shape[0]
    o_kr = MLA_Q_LORA + MLA_KV_LORA
    o_dq = o_kr + MLA_ROPE
    kr_slab = jnp.zeros((D, LANES), F32).at[:, MLA_NOPE:MLA_NOPE + MLA_ROPE].set(w_in[:, o_kr:o_dq])
    w_in_p = jnp.concatenate([w_in[:, :o_kr], kr_slab, w_in[:, o_dq:]], axis=1).astype(BF16)
    assert w_in_p.shape[1] == _EV_END

    r = w_uq.shape[0]
    qh = w_uq.reshape(r, MLA_HEADS, MLA_NOPE + MLA_ROPE)
    qh = jnp.pad(qh, ((0, 0), (0, 0), (0, LANES - MLA_NOPE - MLA_ROPE)))
    w_uq_p = qh.reshape(r, MLA_HEADS * LANES).astype(BF16)

    r = w_ukv.shape[0]
    kvh = w_ukv.reshape(r, MLA_HEADS, MLA_NOPE + MLA_V)
    kn = jnp.pad(kvh[:, :, :MLA_NOPE], ((0, 0), (0, 0), (0, LANES - MLA_NOPE)))
    vh = jnp.pad(kvh[:, :, MLA_NOPE:], ((0, 0), (0, 0), (0, LANES - MLA_V)))
    w_ukv_p = jnp.concatenate([kn.reshape(r, MLA_HEADS * LANES),
                               vh.reshape(r, MLA_HEADS * LANES)], axis=1).astype(BF16)
    return w_in_p, w_uq_p, w_ukv_p


def _prep_odd_weights(w_in):
    D = w_in.shape[0]
    nq = SWA_HEADS * SWA_HD
    nkv = SWA_KV_HEADS * SWA_HD
    dup = lambda w: jnp.concatenate([w.reshape(D, SWA_KV_HEADS, SWA_HD)] * 2, axis=2).reshape(D, 2 * nkv)
    return jnp.concatenate([w_in[:, :nq], dup(w_in[:, nq:nq + nkv]), dup(w_in[:, nq + nkv:])],
                           axis=1).astype(BF16)


def _rope_tables(S):
    half = MLA_ROPE // 2
    inv = ROPE_THETA ** (-np.arange(half, dtype=np.float32) / half)
    ang = np.arange(S, dtype=np.float32)[:, None] * inv[None, :]
    cos, sin = np.cos(ang).astype(np.float32), np.sin(ang).astype(np.float32)
    c = np.zeros((S, LANES), np.float32)
    sn = np.zeros((S, LANES), np.float32)
    sp = np.zeros((S, LANES), np.float32)
    c[:, :MLA_NOPE] = 1.0
    c[:, MLA_NOPE:MLA_NOPE + half] = cos
    c[:, MLA_NOPE + half:MLA_NOPE + 2 * half] = cos
    sn[:, MLA_NOPE:MLA_NOPE + half] = -sin
    sp[:, MLA_NOPE + half:MLA_NOPE + 2 * half] = sin
    return jnp.asarray(c), jnp.asarray(sn), jnp.asarray(sp)


def _alibi_slopes_log2(n):
    return jnp.asarray(2.0 ** (-8.0 * np.arange(1, n + 1, dtype=np.float32) / n) * LOG2E, F32)


def kernel(x, ev_w_in, ev_q_norm, ev_kv_norm, ev_w_uq, ev_w_ukv, ev_lam_q1, ev_lam_k1, ev_lam_q2, ev_lam_k2, ev_diff_norm, ev_w_out, od_w_in, od_sink, od_w_out, ln1_g, ln1_b, ln2_g, ln2_b, ffn_w1, ffn_b1, ffn_w2, ffn_b2):
    B, S, D = x.shape
    row = lambda a: a.reshape(1, -1).astype(F32)
    rope_c, rope_sn, rope_sp = _rope_tables(S)
    xs = x
    for layer in range(DEPTH):
        j = layer // 2
        if layer % 2 == 0:
            lambda_init = 0.8 - 0.6 * math.exp(-0.3 * layer)
            w_in_p, w_uq_p, w_ukv_p = _prep_even_weights(ev_w_in[j], ev_w_uq[j], ev_w_ukv[j])
            q, k, v, dq, dk, dv = _even_in_proj(xs, w_in_p, row(ev_q_norm[j]), row(ev_kv_norm[j]),
                                                w_uq_p, w_ukv_p, rope_c, rope_sn, rope_sp)
            o_mla = _mla_attention(q, k, v)
            lam_vecs = jnp.stack([ev_lam_q1[j], ev_lam_k1[j], ev_lam_q2[j], ev_lam_k2[j]]).astype(F32)
            o_diff = _diff_attention(_alibi_slopes_log2(DIFF_HEADS), lam_vecs, dq, dk, dv,
                                     row(ev_diff_norm[j]), lambda_init)
            n_a = o_mla.shape[-1]
            o_a, o_b, a_blk, b_blk = o_mla.reshape(B * S, n_a), o_diff.reshape(B * S, -1), 0, 0
            w_out = ev_w_out[j]
        else:
            q, k, v = _odd_in_proj(xs, _prep_odd_weights(od_w_in[j]))
            o = _swa_attention(_alibi_slopes_log2(SWA_HEADS), od_sink[j].astype(F32), q, k, v)
            n_a = o.shape[-1] // 2
            o_a = o_b = o.reshape(B * S, -1)
            a_blk, b_blk = 0, 1
            w_out = od_w_out[j]
        xs = _post(o_a, o_b, a_blk, b_blk, xs.reshape(B * S, D),
                   w_out[:n_a].astype(BF16), w_out[n_a:].astype(BF16),
                   row(ln1_g[layer]), row(ln1_b[layer]),
                   ffn_w1[layer].astype(BF16), row(ffn_b1[layer]),
                   ffn_w2[layer].astype(BF16), row(ffn_b2[layer]),
                   row(ln2_g[layer]), row(ln2_b[layer])).reshape(B, S, D)
    return xs
```

```python
import functools
import math

import jax
import jax.numpy as jnp
import numpy as np
from jax import lax
from jax.experimental import pallas as pl
from jax.experimental.pallas import tpu as pltpu

F32 = jnp.float32
BF16 = jnp.bfloat16

D_MODEL = 1024
DEPTH = 2
MLA_HEADS = 8
MLA_NOPE = 64
MLA_ROPE = 32
MLA_V = 64
MLA_Q_LORA = 384
MLA_KV_LORA = 256
ROPE_THETA = 10000.0
DIFF_HEADS = 4
DIFF_HD = 64
DIFF_VD = 2 * DIFF_HD
SWA_HEADS = 16
SWA_KV_HEADS = 4
SWA_HD = 64
SWA_GROUP = SWA_HEADS // SWA_KV_HEADS
WINDOW = 128
BLOCK = 128
D_FF = 4 * D_MODEL
ALPHA = (2 * DEPTH) ** 0.25
LN_EPS = 1e-5
RMS_EPS = 1e-6
LOG2E = math.log2(math.e)

LANES = 128
HALF = LANES // 2

_EV_CQ = 0
_EV_CKV = _EV_CQ + MLA_Q_LORA
_EV_KR = _EV_CKV + MLA_KV_LORA
_EV_DQ = _EV_KR + LANES
_EV_DK = _EV_DQ + DIFF_HEADS * LANES
_EV_DV = _EV_DK + DIFF_HEADS * LANES
_EV_END = _EV_DV + DIFF_HEADS * LANES

VMEM_LIMIT_BYTES = 56 * 1024 * 1024

TOKEN_TILE = 512
FULL_QUERY_TILE = 1024
ATTN_UNIT = 512
ATTN_KEY_CHUNK = 256
ONES_ROWS = 16
FF_CHUNK = 1024

BOUND_SLACK = 1.0 + 2.0 ** -6
MIN_ROW_SUM = 2.0 ** -100


def _params(*sem):
    return pltpu.CompilerParams(dimension_semantics=sem, vmem_limit_bytes=VMEM_LIMIT_BYTES)


def _dot(a, b):
    return jnp.dot(a, b, preferred_element_type=F32)


def _dot_nt(a, b):
    return lax.dot_general(a, b, (((1,), (1,)), ((), ())), preferred_element_type=F32)


def _rms(x, g):
    return x * lax.rsqrt(jnp.mean(x * x, axis=-1, keepdims=True) + RMS_EPS) * g


def _layer_norm(x, g, b):
    mu = jnp.mean(x, axis=-1, keepdims=True)
    xc = x - mu
    var = jnp.mean(xc * xc, axis=-1, keepdims=True)
    return xc * lax.rsqrt(var + LN_EPS) * g + b


def _rope_slab(blk, c, s_next, s_prev):
    nxt = pltpu.roll(blk, LANES - MLA_ROPE // 2, 1)
    prv = pltpu.roll(blk, MLA_ROPE // 2, 1)
    return blk * c + nxt * s_next + prv * s_prev


def _sq_norm(x):
    xf = x.astype(F32)
    return jnp.sum(xf * xf, axis=1, keepdims=True)


def _max_sq_norm_tile(x):
    return jnp.broadcast_to(jnp.max(_sq_norm(x), axis=0, keepdims=True), (8, LANES))


def _row_sq_norms(x):
    xf = x.astype(F32)
    sq = xf * xf
    hi = sq.astype(BF16)
    lo = (sq - hi.astype(F32)).astype(BF16)
    ones = jnp.ones((8, x.shape[1]), BF16)
    return (_dot_nt(ones, hi) + _dot_nt(ones, lo))[0:1]


def _pipelined_units(n_units, n_chunks, qk_chunk, pv_chunk, finish=None):
    results = []
    cur = None
    for stage in range(n_units + 1):
        nxt = [] if stage < n_units else None
        acc = None
        for c in range(n_chunks):
            if nxt is not None:
                nxt.append(qk_chunk(stage, c))
            if cur is not None:
                d = pv_chunk(stage - 1, c, cur)
                acc = d if acc is None else acc + d
        if cur is not None:
            if finish is None:
                results.append(acc)
            else:
                finish(stage - 1, acc)
        cur = nxt
    return results


def _all_rows_healthy(row_sums):
    return jnp.min(jnp.where(row_sums >= MIN_ROW_SUM, 1.0, 0.0)) > 0.5


def _even_in_kernel(x_ref, w_in_ref, qn_ref, kvn_ref, w_uq_ref, w_ukv_ref, c_ref, sn_ref, sp_ref,
                    q_ref, k_ref, v_ref, dq_ref, dk_ref, dv_ref):
    xb = x_ref[0].astype(BF16)
    h = _dot(xb, w_in_ref[...])
    c, sn, sp = c_ref[...], sn_ref[...], sp_ref[...]
    low = lax.broadcasted_iota(jnp.int32, (xb.shape[0], LANES), 1) < HALF

    cq = _rms(h[:, _EV_CQ:_EV_CKV], qn_ref[...])
    q = _dot(cq.astype(BF16), w_uq_ref[...])
    q_scale = (MLA_NOPE + MLA_ROPE) ** -0.5 * LOG2E
    for hd in range(MLA_HEADS):
        blk = q[:, hd * LANES:(hd + 1) * LANES]
        q_ref[0, hd] = (_rope_slab(blk, c, sn, sp) * q_scale).astype(BF16)

    ckv = _rms(h[:, _EV_CKV:_EV_KR], kvn_ref[...])
    kv = _dot(ckv.astype(BF16), w_ukv_ref[...])
    kr = _rope_slab(h[:, _EV_KR:_EV_DQ], c, sn, sp)
    v_off = MLA_HEADS * LANES
    for hd in range(MLA_HEADS):
        k_ref[0, hd] = (kv[:, hd * LANES:(hd + 1) * LANES] + kr).astype(BF16)
        vh = kv[:, v_off + hd * LANES:v_off + (hd + 1) * LANES]
        v_ref[0, hd] = jnp.where(low, vh, 1.0).T.astype(BF16)

    dq_scale = DIFF_HD ** -0.5 * LOG2E
    ones = jnp.ones((ONES_ROWS, xb.shape[0]), BF16)
    for hd in range(DIFF_HEADS):
        dq_ref[0, hd] = (h[:, _EV_DQ + hd * LANES:_EV_DQ + (hd + 1) * LANES] * dq_scale).astype(BF16)
        dk_ref[0, hd] = h[:, _EV_DK + hd * LANES:_EV_DK + (hd + 1) * LANES].astype(BF16)
        dv_ref[0, hd, :LANES, :] = h[:, _EV_DV + hd * LANES:_EV_DV + (hd + 1) * LANES].T.astype(BF16)
        dv_ref[0, hd, LANES:, :] = ones


def _even_in_proj(x, w_in_p, q_norm, kv_norm, w_uq_p, w_ukv_p, rope_c, rope_sn, rope_sp):
    B, S, D = x.shape
    tm = min(TOKEN_TILE, S)
    nt = S // tm
    const = lambda b, i: (0, 0)
    head_out = lambda n, w=LANES: pl.BlockSpec((1, n, tm, w), lambda b, i: (b, 0, i, 0))
    head_shape = lambda n, w=LANES: jax.ShapeDtypeStruct((B, n, S, w), BF16)
    head_out_t = lambda n, r: pl.BlockSpec((1, n, r, tm), lambda b, i: (b, 0, 0, i))
    head_shape_t = lambda n, r: jax.ShapeDtypeStruct((B, n, r, S), BF16)
    return pl.pallas_call(
        _even_in_kernel,
        grid=(B, nt),
        in_specs=[
            pl.BlockSpec((1, tm, D), lambda b, i: (b, i, 0)),
            pl.BlockSpec(w_in_p.shape, const),
            pl.BlockSpec(q_norm.shape, const),
            pl.BlockSpec(kv_norm.shape, const),
            pl.BlockSpec(w_uq_p.shape, const),
            pl.BlockSpec(w_ukv_p.shape, const),
            pl.BlockSpec((tm, LANES), lambda b, i: (i, 0)),
            pl.BlockSpec((tm, LANES), lambda b, i: (i, 0)),
            pl.BlockSpec((tm, LANES), lambda b, i: (i, 0)),
        ],
        out_specs=[head_out(MLA_HEADS), head_out(MLA_HEADS), head_out_t(MLA_HEADS, LANES),
                   head_out(DIFF_HEADS), head_out(DIFF_HEADS), head_out_t(DIFF_HEADS, LANES + ONES_ROWS)],
        out_shape=[head_shape(MLA_HEADS), head_shape(MLA_HEADS), head_shape_t(MLA_HEADS, LANES),
                   head_shape(DIFF_HEADS), head_shape(DIFF_HEADS),
                   head_shape_t(DIFF_HEADS, LANES + ONES_ROWS)],
        compiler_params=_params("parallel", "parallel"),
        name="even_in_proj",
    )(x, w_in_p, q_norm, kv_norm, w_uq_p, w_ukv_p, rope_c, rope_sn, rope_sp)


def _mla_attn_kernel(q_ref, k_ref, vt_ref, o_ref, kmax_ref):
    S, tq = k_ref.shape[2], q_ref.shape[2]
    unit, kc = min(ATTN_UNIT, tq), min(ATTN_KEY_CHUNK, S)

    @pl.when(pl.program_id(2) == 0)
    def _():
        for hh in range(2):
            kmax_ref[hh] = _max_sq_norm_tile(k_ref[0, hh])

    units = [(hh, u) for u in range(tq // unit) for hh in range(2)]

    def q_of(i):
        hh, u = units[i]
        return q_ref[0, hh, u * unit:(u + 1) * unit, :]

    bounds = [jnp.sqrt(_row_sq_norms(q_of(i)) * kmax_ref[units[i][0]][0:1, 0:1]) * BOUND_SLACK
              for i in range(len(units))]

    def qk_chunk(i, c):
        return _dot_nt(k_ref[0, units[i][0], c * kc:(c + 1) * kc, :], q_of(i))

    def pv_chunk(i, c, scores):
        pt = jnp.exp2(scores[c] - bounds[i]).astype(BF16)
        return _dot(vt_ref[0, units[i][0], :, c * kc:(c + 1) * kc], pt)

    def write(accs):
        for u in range(tq // unit):
            ot = jnp.concatenate([a[:HALF] / a[HALF:] for a in accs[2 * u:2 * u + 2]], axis=0)
            o_ref[0, u * unit:(u + 1) * unit, :] = ot.T.astype(o_ref.dtype)

    accs = _pipelined_units(len(units), S // kc, qk_chunk, pv_chunk)
    write(accs)
    sums = accs[0][HALF:HALF + 8]
    for a in accs[1:]:
        sums = jnp.minimum(sums, a[HALF:HALF + 8])
    healthy = _all_rows_healthy(sums)

    @pl.when(jnp.logical_not(healthy))
    def _():
        exact = []
        for i, (hh, _) in enumerate(units):
            st = _dot_nt(k_ref[0, hh], q_of(i))
            pt = jnp.exp2(st - jnp.max(st, axis=0, keepdims=True)).astype(BF16)
            exact.append(_dot(vt_ref[0, hh], pt))
        write(exact)


def _mla_attention(q, k, v):
    B, H, S, _ = q.shape
    tq = min(FULL_QUERY_TILE, S)
    return pl.pallas_call(
        _mla_attn_kernel,
        grid=(B, H // 2, S // tq),
        in_specs=[
            pl.BlockSpec((1, 2, tq, LANES), lambda b, j, i: (b, j, i, 0)),
            pl.BlockSpec((1, 2, S, LANES), lambda b, j, i: (b, j, 0, 0)),
            pl.BlockSpec((1, 2, LANES, S), lambda b, j, i: (b, j, 0, 0)),
        ],
        out_specs=pl.BlockSpec((1, tq, LANES), lambda b, j, i: (b, i, j)),
        out_shape=jax.ShapeDtypeStruct((B, S, H * MLA_V), BF16),
        scratch_shapes=[pltpu.VMEM((2, 8, LANES), F32)],
        compiler_params=_params("parallel", "parallel", "arbitrary"),
        name="mla_attention",
    )(q, k, v)


def _diff_attn_kernel(slope_ref, lam_ref, q_ref, k_ref, vt_ref, g_ref, o_ref, kmax_ref, decay_ref, *,
                      lambda_init):
    hd = pl.program_id(0)
    qi = pl.program_id(2)
    S, tq = k_ref.shape[2], q_ref.shape[2]
    unit, kc = min(ATTN_UNIT, tq), min(ATTN_KEY_CHUNK, S)
    slope = slope_ref[hd]

    min_off = -((S - unit) // kc)

    @pl.when((pl.program_id(1) == 0) & (qi == 0))
    def _():
        rel0 = (lax.broadcasted_iota(jnp.int32, (kc, unit), 0)
                - lax.broadcasted_iota(jnp.int32, (kc, unit), 1))
        for d in range(decay_ref.shape[0]):
            dist = jnp.abs(rel0 + kc * (d + min_off)).astype(F32) * slope
            decay_ref[d] = jnp.exp2(-dist)

    @pl.when(qi == 0)
    def _():
        k = k_ref[0, 0]
        k_lane = lax.broadcasted_iota(jnp.int32, k.shape, 1)
        k_zero = jnp.zeros_like(k)
        kmax_ref[0] = _max_sq_norm_tile(jnp.where(k_lane < HALF, k, k_zero))
        kmax_ref[1] = _max_sq_norm_tile(jnp.where(k_lane >= HALF, k, k_zero))

    lv = lam_ref[...]
    lam = (jnp.exp(jnp.sum(lv[0:1] * lv[1:2], axis=-1, keepdims=True))
           - jnp.exp(jnp.sum(lv[2:3] * lv[3:4], axis=-1, keepdims=True)) + lambda_init)

    units = [(mp, u) for u in range(tq // unit) for mp in range(2)]
    lane = lax.broadcasted_iota(jnp.int32, (unit, LANES), 1)

    def q_of(i):
        mp, u = units[i]
        q = q_ref[0, 0, u * unit:(u + 1) * unit, :]
        return jnp.where((lane < HALF) if mp == 0 else (lane >= HALF), q, jnp.zeros_like(q))

    def q_pos(i):
        u = units[i][1]
        return (qi * tq + u * unit + lax.broadcasted_iota(jnp.int32, (1, unit), 1)).astype(F32) * slope

    def k_pos(c, n):
        return (c + lax.broadcasted_iota(jnp.int32, (n, 1), 0)).astype(F32) * slope

    bounds = [jnp.sqrt(_row_sq_norms(q_of(i)) * kmax_ref[units[i][0]][0:1, 0:1]) * BOUND_SLACK
              for i in range(len(units))]

    def qk_chunk(i, c):
        return _dot_nt(k_ref[0, 0, c * kc:(c + 1) * kc, :], q_of(i))

    def pv_chunk(i, c, scores):
        offset = c - qi * (tq // kc) - units[i][1] * (unit // kc) - min_off
        pt = (jnp.exp2(scores[c] - bounds[i]) * decay_ref[offset]).astype(BF16)
        return _dot(vt_ref[0, 0, :, c * kc:(c + 1) * kc], pt)

    def write(accs):
        for u in range(tq // unit):
            outs = [a[:LANES] / a[LANES:LANES + 1] for a in accs[2 * u:2 * u + 2]]
            o = (outs[0] - lam * outs[1]).T
            o = _rms(o, g_ref[...]) * (1.0 - lambda_init)
            o_ref[0, u * unit:(u + 1) * unit, :] = o.astype(o_ref.dtype)

    accs = _pipelined_units(len(units), S // kc, qk_chunk, pv_chunk)
    write(accs)
    sums = accs[0][LANES:LANES + 8]
    for a in accs[1:]:
        sums = jnp.minimum(sums, a[LANES:LANES + 8])
    healthy = _all_rows_healthy(sums)

    @pl.when(jnp.logical_not(healthy))
    def _():
        exact = []
        for i in range(len(units)):
            st = _dot_nt(k_ref[0, 0], q_of(i)) - jnp.abs(k_pos(0, S) - q_pos(i))
            pt = jnp.exp2(st - jnp.max(st, axis=0, keepdims=True)).astype(BF16)
            exact.append(_dot(vt_ref[0, 0], pt))
        write(exact)


def _diff_attention(slopes, lam_vecs, q, k, v, diff_norm, lambda_init):
    B, H, S, _ = q.shape
    tq = min(FULL_QUERY_TILE, S)
    unit, kc = min(ATTN_UNIT, tq), min(ATTN_KEY_CHUNK, S)
    smem = pl.BlockSpec(memory_space=pltpu.SMEM)
    return pl.pallas_call(
        functools.partial(_diff_attn_kernel, lambda_init=lambda_init),
        grid=(H, B, S // tq),
        in_specs=[
            smem,
            pl.BlockSpec(lam_vecs.shape, lambda h, b, i: (0, 0)),
            pl.BlockSpec((1, 1, tq, LANES), lambda h, b, i: (b, h, i, 0)),
            pl.BlockSpec((1, 1, S, LANES), lambda h, b, i: (b, h, 0, 0)),
            pl.BlockSpec((1, 1, LANES + ONES_ROWS, S), lambda h, b, i: (b, h, 0, 0)),
            pl.BlockSpec(diff_norm.shape, lambda h, b, i: (0, 0)),
        ],
        out_specs=pl.BlockSpec((1, tq, LANES), lambda h, b, i: (b, i, h)),
        out_shape=jax.ShapeDtypeStruct((B, S, H * DIFF_VD), BF16),
        scratch_shapes=[pltpu.VMEM((2, 8, LANES), F32),
                        pltpu.VMEM(((S - unit) // kc + S // kc, kc, unit), F32)],
        compiler_params=_params("arbitrary", "arbitrary", "arbitrary"),
        name="diff_attention",
    )(slopes, lam_vecs, q, k, v, diff_norm)


def _odd_in_kernel(x_ref, w_ref, q_ref, k_ref, v_ref):
    h = _dot(x_ref[0].astype(BF16), w_ref[...])
    nq = SWA_HEADS // 2
    q_scale = SWA_HD ** -0.5 * LOG2E
    ones = jnp.ones((ONES_ROWS, h.shape[0]), BF16)
    for pr in range(nq):
        q_ref[0, pr] = (h[:, pr * LANES:(pr + 1) * LANES] * q_scale).astype(BF16)
    for g in range(SWA_KV_HEADS):
        k_ref[0, g] = h[:, (nq + g) * LANES:(nq + g + 1) * LANES].astype(BF16)
        v_lo = (nq + SWA_KV_HEADS + g) * LANES
        v_ref[0, g, :SWA_HD, :] = h[:, v_lo:v_lo + LANES].T[:SWA_HD].astype(BF16)
        v_ref[0, g, SWA_HD:, :] = ones


def _odd_in_proj(x, w_p):
    B, S, D = x.shape
    tm = min(TOKEN_TILE, S)
    head_out = lambda n, w=LANES: pl.BlockSpec((1, n, tm, w), lambda b, i: (b, 0, i, 0))
    head_shape = lambda n, w=LANES: jax.ShapeDtypeStruct((B, n, S, w), BF16)
    return pl.pallas_call(
        _odd_in_kernel,
        grid=(B, S // tm),
        in_specs=[
            pl.BlockSpec((1, tm, D), lambda b, i: (b, i, 0)),
            pl.BlockSpec(w_p.shape, lambda b, i: (0, 0)),
        ],
        out_specs=[head_out(SWA_HEADS // 2), head_out(SWA_KV_HEADS),
                   pl.BlockSpec((1, SWA_KV_HEADS, SWA_HD + ONES_ROWS, tm), lambda b, i: (b, 0, 0, i))],
        out_shape=[head_shape(SWA_HEADS // 2), head_shape(SWA_KV_HEADS),
                   jax.ShapeDtypeStruct((B, SWA_KV_HEADS, SWA_HD + ONES_ROWS, S), BF16)],
        compiler_params=_params("parallel", "parallel"),
        name="odd_in_proj",
    )(x, w_p)


SWA_UNIT_BLOCKS = 2
SWA_UNIT_SPAN = SWA_UNIT_BLOCKS * BLOCK + 2 * WINDOW
SWA_KEY_CHUNK = 256
_SWA_SHIFTS = (-WINDOW, 0, -2 * WINDOW)


def _swa_attn_kernel(slope_ref, sink_ref, q_ref, k_ref, vt_ref, o_ref, bias_ref):
    g = pl.program_id(0)
    S = k_ref.shape[2]
    span = SWA_UNIT_SPAN
    n_units = S // (SWA_UNIT_BLOCKS * BLOCK)
    kc = SWA_KEY_CHUNK
    n_chunks = span // kc
    cols = SWA_UNIT_BLOCKS * SWA_GROUP * BLOCK
    low = lax.broadcasted_iota(jnp.int32, (BLOCK, LANES), 1) < HALF
    col = lax.broadcasted_iota(jnp.int32, (1, cols), 1)
    head_of_col = (col // BLOCK) % SWA_GROUP
    slope_row = jnp.zeros((1, cols), F32)
    sink_row = jnp.zeros((1, cols), F32)
    for hh in range(SWA_GROUP):
        slope_row = jnp.where(head_of_col == hh, slope_ref[g * SWA_GROUP + hh], slope_row)
        sink_row = jnp.where(head_of_col == hh, sink_ref[g * SWA_GROUP + hh] * LOG2E, sink_row)

    @pl.when(pl.program_id(1) == 0)
    def _():
        q_in_unit = col // (SWA_GROUP * BLOCK) * BLOCK + col % BLOCK
        rel0 = lax.broadcasted_iota(jnp.int32, (span, cols), 0) - q_in_unit
        for t, shift in enumerate(_SWA_SHIFTS):
            dist = jnp.abs(rel0 + shift)
            bias_ref[t] = jnp.where(dist <= WINDOW, -slope_row * dist.astype(F32), -jnp.inf)

    def first_query(u):
        return u * SWA_UNIT_BLOCKS * BLOCK

    def window_start(u):
        return min(max(first_query(u) - WINDOW, 0), S - span)

    def stacked_q(u):
        parts = []
        for blk in range(SWA_UNIT_BLOCKS):
            lo = first_query(u) + blk * BLOCK
            for pr in range(2):
                qp = q_ref[0, pr, lo:lo + BLOCK, :]
                zero = jnp.zeros_like(qp)
                parts += [jnp.where(low, qp, zero), jnp.where(low, zero, qp)]
        return jnp.concatenate(parts, axis=0)

    col_max = {}

    def qk_chunk(u, c):
        lo = window_start(u) + c * kc
        t = _SWA_SHIFTS.index(window_start(u) - first_query(u))
        st = _dot_nt(k_ref[0, 0, lo:lo + kc, :], stacked_q(u)) + bias_ref[t, c * kc:(c + 1) * kc, :]
        col_max[u] = jnp.maximum(col_max.get(u, sink_row), jnp.max(st, axis=0, keepdims=True))
        return st

    def pv_chunk(u, c, scores):
        lo = window_start(u) + c * kc
        return _dot(vt_ref[0, 0, :, lo:lo + kc], jnp.exp2(scores[c] - col_max[u]).astype(BF16))

    def finish(u, acc):
        m = col_max.pop(u)
        ot = acc[:SWA_HD] / (acc[SWA_HD:SWA_HD + 1] + jnp.exp2(sink_row - m))
        for blk in range(SWA_UNIT_BLOCKS):
            lo = first_query(u) + blk * BLOCK
            for pr in range(2):
                c0 = (blk * SWA_GROUP + 2 * pr) * BLOCK
                pair = jnp.concatenate([ot[:, c0:c0 + BLOCK], ot[:, c0 + BLOCK:c0 + 2 * BLOCK]], axis=0)
                o_ref[0, lo:lo + BLOCK, pr * LANES:(pr + 1) * LANES] = pair.T.astype(o_ref.dtype)

    _pipelined_units(n_units, n_chunks, qk_chunk, pv_chunk, finish)


def _swa_attention(slopes, sink, q, k, v):
    B, _, S, _ = q.shape
    smem = pl.BlockSpec(memory_space=pltpu.SMEM)
    return pl.pallas_call(
        _swa_attn_kernel,
        grid=(SWA_KV_HEADS, B),
        in_specs=[
            smem, smem,
            pl.BlockSpec((1, 2, S, LANES), lambda g, b: (b, g, 0, 0)),
            pl.BlockSpec((1, 1, S, LANES), lambda g, b: (b, g, 0, 0)),
            pl.BlockSpec((1, 1, SWA_HD + ONES_ROWS, S), lambda g, b: (b, g, 0, 0)),
        ],
        out_specs=pl.BlockSpec((1, S, 2 * LANES), lambda g, b: (b, 0, g)),
        out_shape=jax.ShapeDtypeStruct((B, S, SWA_HEADS * SWA_HD), BF16),
        scratch_shapes=[pltpu.VMEM((len(_SWA_SHIFTS), SWA_UNIT_SPAN,
                                    SWA_UNIT_BLOCKS * SWA_GROUP * BLOCK), F32)],
        compiler_params=_params("arbitrary", "arbitrary"),
        name="swa_attention",
    )(slopes, sink, q, k, v)


def _post_kernel(oa_ref, ob_ref, x_ref, woa_ref, wob_ref, g1_ref, b1_ref,
                 w1_ref, fb1_ref, w2_ref, fb2_ref, g2_ref, b2_ref, out_ref):
    y = _dot(oa_ref[...], woa_ref[...]) + _dot(ob_ref[...], wob_ref[...])
    x1 = _layer_norm(ALPHA * x_ref[...] + y, g1_ref[...], b1_ref[...])
    x1b = x1.astype(BF16)
    acc = jnp.zeros_like(x1) + fb2_ref[...]
    for c in range(0, w1_ref.shape[1], FF_CHUNK):
        hc = _dot(x1b, w1_ref[:, c:c + FF_CHUNK]) + fb1_ref[:, c:c + FF_CHUNK]
        hc = jnp.square(jnp.maximum(hc, 0.0))
        acc = acc + _dot(hc.astype(BF16), w2_ref[c:c + FF_CHUNK, :])
    out_ref[...] = _layer_norm(ALPHA * x1 + acc, g2_ref[...], b2_ref[...])


def _post(o_a, o_b, a_blk, b_blk, x, wo_a, wo_b, g1, b1, w1, fb1, w2, fb2, g2, b2):
    T, D = x.shape
    tm = min(TOKEN_TILE, T)
    half = wo_a.shape[0]
    const = lambda i: (0, 0)
    resident = lambda a: pl.BlockSpec(a.shape, const, pipeline_mode=pl.Buffered(1))
    return pl.pallas_call(
        _post_kernel,
        grid=(T // tm,),
        in_specs=[
            pl.BlockSpec((tm, half), lambda i: (i, a_blk)),
            pl.BlockSpec((tm, half), lambda i: (i, b_blk)),
            pl.BlockSpec((tm, D), lambda i: (i, 0)),
            resident(wo_a), resident(wo_b), resident(g1), resident(b1),
            resident(w1), resident(fb1), resident(w2), resident(fb2), resident(g2), resident(b2),
        ],
        out_specs=pl.BlockSpec((tm, D), lambda i: (i, 0)),
        out_shape=jax.ShapeDtypeStruct((T, D), F32),
        compiler_params=_params("parallel"),
        name="post_mlp",
    )(o_a, o_b, x, wo_a, wo_b, g1, b1, w1, fb1, w2, fb2, g2, b2)


def _prep_even_weights(w_in, w_uq, w_ukv):
    D = w_in.shape[0]
    o_kr = MLA_Q_LORA + MLA_KV_LORA
    o_dq = o_kr + MLA_ROPE
    kr_slab = jnp.zeros((D, LANES), F32).at[:, MLA_NOPE:MLA_NOPE + MLA_ROPE].set(w_in[:, o_kr:o_dq])
    w_in_p = jnp.concatenate([w_in[:, :o_kr], kr_slab, w_in[:, o_dq:]], axis=1).astype(BF16)
    assert w_in_p.shape[1] == _EV_END

    r = w_uq.shape[0]
    qh = w_uq.reshape(r, MLA_HEADS, MLA_NOPE + MLA_ROPE)
    qh = jnp.pad(qh, ((0, 0), (0, 0), (0, LANES - MLA_NOPE - MLA_ROPE)))
    w_uq_p = qh.reshape(r, MLA_HEADS * LANES).astype(BF16)

    r = w_ukv.shape[0]
    kvh = w_ukv.reshape(r, MLA_HEADS, MLA_NOPE + MLA_V)
    kn = jnp.pad(kvh[:, :, :MLA_NOPE], ((0, 0), (0, 0), (0, LANES - MLA_NOPE)))
    vh = jnp.pad(kvh[:, :, MLA_NOPE:], ((0, 0), (0, 0), (0, LANES - MLA_V)))
    w_ukv_p = jnp.concatenate([kn.reshape(r, MLA_HEADS * LANES),
                               vh.reshape(r, MLA_HEADS * LANES)], axis=1).astype(BF16)
    return w_in_p, w_uq_p, w_ukv_p


def _prep_odd_weights(w_in):
    D = w_in.shape[0]
    nq = SWA_HEADS * SWA_HD
    nkv = SWA_KV_HEADS * SWA_HD
    dup = lambda w: jnp.concatenate([w.reshape(D, SWA_KV_HEADS, SWA_HD)] * 2, axis=2).reshape(D, 2 * nkv)
    return jnp.concatenate([w_in[:, :nq], dup(w_in[:, nq:nq + nkv]), dup(w_in[:, nq + nkv:])],
                           axis=1).astype(BF16)


def _rope_tables(S):
    half = MLA_ROPE // 2
    inv = ROPE_THETA ** (-np.arange(half, dtype=np.float32) / half)
    ang = np.arange(S, dtype=np.float32)[:, None] * inv[None, :]
    cos, sin = np.cos(ang).astype(np.float32), np.sin(ang).astype(np.float32)
    c = np.zeros((S, LANES), np.float32)
    sn = np.zeros((S, LANES), np.float32)
    sp = np.zeros((S, LANES), np.float32)
    c[:, :MLA_NOPE] = 1.0
    c[:, MLA_NOPE:MLA_NOPE + half] = cos
    c[:, MLA_NOPE + half:MLA_NOPE + 2 * half] = cos
    sn[:, MLA_NOPE:MLA_NOPE + half] = -sin
    sp[:, MLA_NOPE + half:MLA_NOPE + 2 * half] = sin
    return jnp.asarray(c), jnp.asarray(sn), jnp.asarray(sp)


def _alibi_slopes_log2(n):
    return jnp.asarray(2.0 ** (-8.0 * np.arange(1, n + 1, dtype=np.float32) / n) * LOG2E, F32)


def kernel(x, ev_w_in, ev_q_norm, ev_kv_norm, ev_w_uq, ev_w_ukv, ev_lam_q1, ev_lam_k1, ev_lam_q2, ev_lam_k2, ev_diff_norm, ev_w_out, od_w_in, od_sink, od_w_out, ln1_g, ln1_b, ln2_g, ln2_b, ffn_w1, ffn_b1, ffn_w2, ffn_b2):
    B, S, D = x.shape
    row = lambda a: a.reshape(1, -1).astype(F32)
    rope_c, rope_sn, rope_sp = _rope_tables(S)
    xs = x
    for layer in range(DEPTH):
        j = layer // 2
        if layer % 2 == 0:
            lambda_init = 0.8 - 0.6 * math.exp(-0.3 * layer)
            w_in_p, w_uq_p, w_ukv_p = _prep_even_weights(ev_w_in[j], ev_w_uq[j], ev_w_ukv[j])
            q, k, v, dq, dk, dv = _even_in_proj(xs, w_in_p, row(ev_q_norm[j]), row(ev_kv_norm[j]),
                                                w_uq_p, w_ukv_p, rope_c, rope_sn, rope_sp)
            o_mla = _mla_attention(q, k, v)
            lam_vecs = jnp.stack([ev_lam_q1[j], ev_lam_k1[j], ev_lam_q2[j], ev_lam_k2[j]]).astype(F32)
            o_diff = _diff_attention(_alibi_slopes_log2(DIFF_HEADS), lam_vecs, dq, dk, dv,
                                     row(ev_diff_norm[j]), lambda_init)
            n_a = o_mla.shape[-1]
            o_a, o_b, a_blk, b_blk = o_mla.reshape(B * S, n_a), o_diff.reshape(B * S, -1), 0, 0
            w_out = ev_w_out[j]
        else:
            q, k, v = _odd_in_proj(xs, _prep_odd_weights(od_w_in[j]))
            o = _swa_attention(_alibi_slopes_log2(SWA_HEADS), od_sink[j].astype(F32), q, k, v)
            n_a = o.shape[-1] // 2
            o_a = o_b = o.reshape(B * S, -1)
            a_blk, b_blk = 0, 1
            w_out = od_w_out[j]
        xs = _post(o_a, o_b, a_blk, b_blk, xs.reshape(B * S, D),
                   w_out[:n_a].astype(BF16), w_out[n_a:].astype(BF16),
                   row(ln1_g[layer]), row(ln1_b[layer]),
                   ffn_w1[layer].astype(BF16), row(ffn_b1[layer]),
                   ffn_w2[layer].astype(BF16), row(ffn_b2[layer]),
                   row(ln2_g[layer]), row(ln2_b[layer])).reshape(B, S, D)
    return xs
```

```python
import functools
import math

import jax
import jax.numpy as jnp
import numpy as np
from jax import lax
from jax.experimental import pallas as pl
from jax.experimental.pallas import tpu as pltpu

F32 = jnp.float32
BF16 = jnp.bfloat16

D_MODEL = 1024
DEPTH = 2
MLA_HEADS = 8
MLA_NOPE = 64
MLA_ROPE = 32
MLA_V = 64
MLA_Q_LORA = 384
MLA_KV_LORA = 256
ROPE_THETA = 10000.0
DIFF_HEADS = 4
DIFF_HD = 64
DIFF_VD = 2 * DIFF_HD
SWA_HEADS = 16
SWA_KV_HEADS = 4
SWA_HD = 64
SWA_GROUP = SWA_HEADS // SWA_KV_HEADS
WINDOW = 128
BLOCK = 128
D_FF = 4 * D_MODEL
ALPHA = (2 * DEPTH) ** 0.25
LN_EPS = 1e-5
RMS_EPS = 1e-6
LOG2E = math.log2(math.e)

LANES = 128
HALF = LANES // 2

_EV_CQ = 0
_EV_CKV = _EV_CQ + MLA_Q_LORA
_EV_KR = _EV_CKV + MLA_KV_LORA
_EV_DQ = _EV_KR + LANES
_EV_DK = _EV_DQ + DIFF_HEADS * LANES
_EV_DV = _EV_DK + DIFF_HEADS * LANES
_EV_END = _EV_DV + DIFF_HEADS * LANES

VMEM_LIMIT_BYTES = 56 * 1024 * 1024

TOKEN_TILE = 512
FULL_QUERY_TILE = 2048
ATTN_UNIT = 512
ATTN_KEY_CHUNK = 256
ONES_ROWS = 16
FF_CHUNK = 1024
POST_SPLIT = 2

BOUND_SLACK = 1.0 + 2.0 ** -6
MIN_ROW_SUM = 2.0 ** -100


def _params(*sem):
    return pltpu.CompilerParams(dimension_semantics=sem, vmem_limit_bytes=VMEM_LIMIT_BYTES)


def _dot(a, b):
    return jnp.dot(a, b, preferred_element_type=F32)


def _dot_nt(a, b):
    return lax.dot_general(a, b, (((1,), (1,)), ((), ())), preferred_element_type=F32)


def _rms(x, g):
    return x * lax.rsqrt(jnp.mean(x * x, axis=-1, keepdims=True) + RMS_EPS) * g


def _layer_norm(x, g, b):
    mu = jnp.mean(x, axis=-1, keepdims=True)
    xc = x - mu
    var = jnp.mean(xc * xc, axis=-1, keepdims=True)
    return xc * lax.rsqrt(var + LN_EPS) * g + b


def _rope_slab(blk, c, s_next, s_prev):
    nxt = pltpu.roll(blk, LANES - MLA_ROPE // 2, 1)
    prv = pltpu.roll(blk, MLA_ROPE // 2, 1)
    return blk * c + nxt * s_next + prv * s_prev


def _sq_norm(x):
    xf = x.astype(F32)
    return jnp.sum(xf * xf, axis=1, keepdims=True)


def _max_sq_norm_tile(x):
    return jnp.broadcast_to(jnp.max(_sq_norm(x), axis=0, keepdims=True), (8, LANES))


def _row_sq_norms(x):
    xf = x.astype(F32)
    sq = xf * xf
    hi = sq.astype(BF16)
    lo = (sq - hi.astype(F32)).astype(BF16)
    ones = jnp.ones((8, x.shape[1]), BF16)
    return (_dot_nt(ones, hi) + _dot_nt(ones, lo))[0:1]


def _pipelined_units(n_units, n_chunks, qk_chunk, pv_chunk, finish=None):
    results = []
    cur = None
    for stage in range(n_units + 1):
        nxt = [] if stage < n_units else None
        acc = None
        for c in range(n_chunks):
            if nxt is not None:
                nxt.append(qk_chunk(stage, c))
            if cur is not None:
                d = pv_chunk(stage - 1, c, cur)
                acc = d if acc is None else acc + d
        if cur is not None:
            if finish is None:
                results.append(acc)
            else:
                finish(stage - 1, acc)
        cur = nxt
    return results


def _all_rows_healthy(row_sums):
    return jnp.min(jnp.where(row_sums >= MIN_ROW_SUM, 1.0, 0.0)) > 0.5


def _even_in_kernel(x_ref, w_in_ref, qn_ref, kvn_ref, w_uq_ref, w_ukv_ref, c_ref, sn_ref, sp_ref,
                    q_ref, k_ref, v_ref, dq_ref, dk_ref, dv_ref):
    xb = x_ref[0].astype(BF16)
    h = _dot(xb, w_in_ref[...])
    c, sn, sp = c_ref[...], sn_ref[...], sp_ref[...]
    low = lax.broadcasted_iota(jnp.int32, (xb.shape[0], LANES), 1) < HALF

    cq = _rms(h[:, _EV_CQ:_EV_CKV], qn_ref[...])
    q = _dot(cq.astype(BF16), w_uq_ref[...])
    q_scale = (MLA_NOPE + MLA_ROPE) ** -0.5 * LOG2E
    for hd in range(MLA_HEADS):
        blk = q[:, hd * LANES:(hd + 1) * LANES]
        q_ref[0, hd] = (_rope_slab(blk, c, sn, sp) * q_scale).astype(BF16)

    ckv = _rms(h[:, _EV_CKV:_EV_KR], kvn_ref[...])
    kv = _dot(ckv.astype(BF16), w_ukv_ref[...])
    kr = _rope_slab(h[:, _EV_KR:_EV_DQ], c, sn, sp)
    v_off = MLA_HEADS * LANES
    for hd in range(MLA_HEADS):
        k_ref[0, hd] = (kv[:, hd * LANES:(hd + 1) * LANES] + kr).astype(BF16)
        vh = kv[:, v_off + hd * LANES:v_off + (hd + 1) * LANES]
        v_ref[0, hd] = jnp.where(low, vh, 1.0).T.astype(BF16)

    dq_scale = DIFF_HD ** -0.5 * LOG2E
    ones = jnp.ones((ONES_ROWS, xb.shape[0]), BF16)
    for hd in range(DIFF_HEADS):
        dq_ref[0, hd] = (h[:, _EV_DQ + hd * LANES:_EV_DQ + (hd + 1) * LANES] * dq_scale).astype(BF16)
        dk_ref[0, hd] = h[:, _EV_DK + hd * LANES:_EV_DK + (hd + 1) * LANES].astype(BF16)
        dv_ref[0, hd, :LANES, :] = h[:, _EV_DV + hd * LANES:_EV_DV + (hd + 1) * LANES].T.astype(BF16)
        dv_ref[0, hd, LANES:, :] = ones


def _even_in_proj(x, w_in_p, q_norm, kv_norm, w_uq_p, w_ukv_p, rope_c, rope_sn, rope_sp):
    B, S, D = x.shape
    tm = min(TOKEN_TILE, S)
    nt = S // tm
    const = lambda b, i: (0, 0)
    head_out = lambda n, w=LANES: pl.BlockSpec((1, n, tm, w), lambda b, i: (b, 0, i, 0))
    head_shape = lambda n, w=LANES: jax.ShapeDtypeStruct((B, n, S, w), BF16)
    head_out_t = lambda n, r: pl.BlockSpec((1, n, r, tm), lambda b, i: (b, 0, 0, i))
    head_shape_t = lambda n, r: jax.ShapeDtypeStruct((B, n, r, S), BF16)
    return pl.pallas_call(
        _even_in_kernel,
        grid=(B, nt),
        in_specs=[
            pl.BlockSpec((1, tm, D), lambda b, i: (b, i, 0)),
            pl.BlockSpec(w_in_p.shape, const),
            pl.BlockSpec(q_norm.shape, const),
            pl.BlockSpec(kv_norm.shape, const),
            pl.BlockSpec(w_uq_p.shape, const),
            pl.BlockSpec(w_ukv_p.shape, const),
            pl.BlockSpec((tm, LANES), lambda b, i: (i, 0)),
            pl.BlockSpec((tm, LANES), lambda b, i: (i, 0)),
            pl.BlockSpec((tm, LANES), lambda b, i: (i, 0)),
        ],
        out_specs=[head_out(MLA_HEADS), head_out(MLA_HEADS), head_out_t(MLA_HEADS, LANES),
                   head_out(DIFF_HEADS), head_out(DIFF_HEADS), head_out_t(DIFF_HEADS, LANES + ONES_ROWS)],
        out_shape=[head_shape(MLA_HEADS), head_shape(MLA_HEADS), head_shape_t(MLA_HEADS, LANES),
                   head_shape(DIFF_HEADS), head_shape(DIFF_HEADS),
                   head_shape_t(DIFF_HEADS, LANES + ONES_ROWS)],
        compiler_params=_params("parallel", "parallel"),
        name="even_in_proj",
    )(x, w_in_p, q_norm, kv_norm, w_uq_p, w_ukv_p, rope_c, rope_sn, rope_sp)


def _mla_attn_kernel(q_ref, k_ref, vt_ref, o_ref, kmax_ref):
    S, tq = k_ref.shape[2], q_ref.shape[2]
    unit, kc = min(ATTN_UNIT, tq), min(ATTN_KEY_CHUNK, S)

    @pl.when(pl.program_id(2) == 0)
    def _():
        for hh in range(2):
            kmax_ref[hh] = _max_sq_norm_tile(k_ref[0, hh])

    units = [(hh, u) for u in range(tq // unit) for hh in range(2)]

    def q_of(i):
        hh, u = units[i]
        return q_ref[0, hh, u * unit:(u + 1) * unit, :]

    bounds = [jnp.sqrt(_row_sq_norms(q_of(i)) * kmax_ref[units[i][0]][0:1, 0:1]) * BOUND_SLACK
              for i in range(len(units))]

    def qk_chunk(i, c):
        return _dot_nt(k_ref[0, units[i][0], c * kc:(c + 1) * kc, :], q_of(i))

    def pv_chunk(i, c, scores):
        pt = jnp.exp2(scores[c] - bounds[i]).astype(BF16)
        return _dot(vt_ref[0, units[i][0], :, c * kc:(c + 1) * kc], pt)

    def write(accs):
        for u in range(tq // unit):
            ot = jnp.concatenate([a[:HALF] / a[HALF:] for a in accs[2 * u:2 * u + 2]], axis=0)
            o_ref[0, u * unit:(u + 1) * unit, :] = ot.T.astype(o_ref.dtype)

    accs = _pipelined_units(len(units), S // kc, qk_chunk, pv_chunk)
    write(accs)
    sums = accs[0][HALF:HALF + 8]
    for a in accs[1:]:
        sums = jnp.minimum(sums, a[HALF:HALF + 8])
    healthy = _all_rows_healthy(sums)

    @pl.when(jnp.logical_not(healthy))
    def _():
        exact = []
        for i, (hh, _) in enumerate(units):
            st = _dot_nt(k_ref[0, hh], q_of(i))
            pt = jnp.exp2(st - jnp.max(st, axis=0, keepdims=True)).astype(BF16)
            exact.append(_dot(vt_ref[0, hh], pt))
        write(exact)


def _mla_attention(q, k, v):
    B, H, S, _ = q.shape
    tq = min(FULL_QUERY_TILE, S)
    return pl.pallas_call(
        _mla_attn_kernel,
        grid=(B, H // 2, S // tq),
        in_specs=[
            pl.BlockSpec((1, 2, tq, LANES), lambda b, j, i: (b, j, i, 0)),
            pl.BlockSpec((1, 2, S, LANES), lambda b, j, i: (b, j, 0, 0)),
            pl.BlockSpec((1, 2, LANES, S), lambda b, j, i: (b, j, 0, 0)),
        ],
        out_specs=pl.BlockSpec((1, tq, LANES), lambda b, j, i: (b, i, j)),
        out_shape=jax.ShapeDtypeStruct((B, S, H * MLA_V), BF16),
        scratch_shapes=[pltpu.VMEM((2, 8, LANES), F32)],
        compiler_params=_params("parallel", "parallel", "arbitrary"),
        name="mla_attention",
    )(q, k, v)


def _diff_attn_kernel(slope_ref, lam_ref, q_ref, k_ref, vt_ref, g_ref, o_ref, kmax_ref, *, lambda_init):
    hd = pl.program_id(1)
    qi = pl.program_id(2)
    S, tq = k_ref.shape[2], q_ref.shape[2]
    unit, kc = min(ATTN_UNIT, tq), min(ATTN_KEY_CHUNK, S)
    slope = slope_ref[hd]

    @pl.when(qi == 0)
    def _():
        k = k_ref[0, 0]
        k_lane = lax.broadcasted_iota(jnp.int32, k.shape, 1)
        k_zero = jnp.zeros_like(k)
        kmax_ref[0] = _max_sq_norm_tile(jnp.where(k_lane < HALF, k, k_zero))
        kmax_ref[1] = _max_sq_norm_tile(jnp.where(k_lane >= HALF, k, k_zero))

    lv = lam_ref[...]
    lam = (jnp.exp(jnp.sum(lv[0:1] * lv[1:2], axis=-1, keepdims=True))
           - jnp.exp(jnp.sum(lv[2:3] * lv[3:4], axis=-1, keepdims=True)) + lambda_init)

    units = [(mp, u) for u in range(tq // unit) for mp in range(2)]
    lane = lax.broadcasted_iota(jnp.int32, (unit, LANES), 1)

    def q_of(i):
        mp, u = units[i]
        q = q_ref[0, 0, u * unit:(u + 1) * unit, :]
        return jnp.where((lane < HALF) if mp == 0 else (lane >= HALF), q, jnp.zeros_like(q))

    def q_pos(i):
        u = units[i][1]
        return (qi * tq + u * unit + lax.broadcasted_iota(jnp.int32, (1, unit), 1)).astype(F32) * slope

    def k_pos(c, n):
        return (c + lax.broadcasted_iota(jnp.int32, (n, 1), 0)).astype(F32) * slope

    bounds = [jnp.sqrt(_row_sq_norms(q_of(i)) * kmax_ref[units[i][0]][0:1, 0:1]) * BOUND_SLACK
              for i in range(len(units))]

    def qk_chunk(i, c):
        return _dot_nt(k_ref[0, 0, c * kc:(c + 1) * kc, :], q_of(i))

    def pv_chunk(i, c, scores):
        dist = jnp.abs(k_pos(c * kc, kc) - q_pos(i))
        pt = jnp.exp2(scores[c] - dist - bounds[i]).astype(BF16)
        return _dot(vt_ref[0, 0, :, c * kc:(c + 1) * kc], pt)

    def write(accs):
        for u in range(tq // unit):
            outs = [a[:LANES] / a[LANES:LANES + 1] for a in accs[2 * u:2 * u + 2]]
            o = (outs[0] - lam * outs[1]).T
            o = _rms(o, g_ref[...]) * (1.0 - lambda_init)
            o_ref[0, u * unit:(u + 1) * unit, :] = o.astype(o_ref.dtype)

    accs = _pipelined_units(len(units), S // kc, qk_chunk, pv_chunk)
    write(accs)
    sums = accs[0][LANES:LANES + 8]
    for a in accs[1:]:
        sums = jnp.minimum(sums, a[LANES:LANES + 8])
    healthy = _all_rows_healthy(sums)

    @pl.when(jnp.logical_not(healthy))
    def _():
        exact = []
        for i in range(len(units)):
            st = _dot_nt(k_ref[0, 0], q_of(i)) - jnp.abs(k_pos(0, S) - q_pos(i))
            pt = jnp.exp2(st - jnp.max(st, axis=0, keepdims=True)).astype(BF16)
            exact.append(_dot(vt_ref[0, 0], pt))
        write(exact)


def _diff_attention(slopes, lam_vecs, q, k, v, diff_norm, lambda_init):
    B, H, S, _ = q.shape
    tq = min(FULL_QUERY_TILE, S)
    smem = pl.BlockSpec(memory_space=pltpu.SMEM)
    return pl.pallas_call(
        functools.partial(_diff_attn_kernel, lambda_init=lambda_init),
        grid=(B, H, S // tq),
        in_specs=[
            smem,
            pl.BlockSpec(lam_vecs.shape, lambda b, h, i: (0, 0)),
            pl.BlockSpec((1, 1, tq, LANES), lambda b, h, i: (b, h, i, 0)),
            pl.BlockSpec((1, 1, S, LANES), lambda b, h, i: (b, h, 0, 0)),
            pl.BlockSpec((1, 1, LANES + ONES_ROWS, S), lambda b, h, i: (b, h, 0, 0)),
            pl.BlockSpec(diff_norm.shape, lambda b, h, i: (0, 0)),
        ],
        out_specs=pl.BlockSpec((1, tq, LANES), lambda b, h, i: (b, i, h)),
        out_shape=jax.ShapeDtypeStruct((B, S, H * DIFF_VD), BF16),
        scratch_shapes=[pltpu.VMEM((2, 8, LANES), F32)],
        compiler_params=_params("parallel", "parallel", "arbitrary"),
        name="diff_attention",
    )(slopes, lam_vecs, q, k, v, diff_norm)


def _odd_in_kernel(x_ref, w_ref, q_ref, k_ref, v_ref):
    h = _dot(x_ref[0].astype(BF16), w_ref[...])
    nq = SWA_HEADS // 2
    q_scale = SWA_HD ** -0.5 * LOG2E
    ones = jnp.ones((ONES_ROWS, h.shape[0]), BF16)
    for pr in range(nq):
        q_ref[0, pr] = (h[:, pr * LANES:(pr + 1) * LANES] * q_scale).astype(BF16)
    for g in range(SWA_KV_HEADS):
        k_ref[0, g] = h[:, (nq + g) * LANES:(nq + g + 1) * LANES].astype(BF16)
        v_lo = (nq + SWA_KV_HEADS + g) * LANES
        v_ref[0, g, :SWA_HD, :] = h[:, v_lo:v_lo + LANES].T[:SWA_HD].astype(BF16)
        v_ref[0, g, SWA_HD:, :] = ones


def _odd_in_proj(x, w_p):
    B, S, D = x.shape
    tm = min(TOKEN_TILE, S)
    head_out = lambda n, w=LANES: pl.BlockSpec((1, n, tm, w), lambda b, i: (b, 0, i, 0))
    head_shape = lambda n, w=LANES: jax.ShapeDtypeStruct((B, n, S, w), BF16)
    return pl.pallas_call(
        _odd_in_kernel,
        grid=(B, S // tm),
        in_specs=[
            pl.BlockSpec((1, tm, D), lambda b, i: (b, i, 0)),
            pl.BlockSpec(w_p.shape, lambda b, i: (0, 0)),
        ],
        out_specs=[head_out(SWA_HEADS // 2), head_out(SWA_KV_HEADS),
                   pl.BlockSpec((1, SWA_KV_HEADS, SWA_HD + ONES_ROWS, tm), lambda b, i: (b, 0, 0, i))],
        out_shape=[head_shape(SWA_HEADS // 2), head_shape(SWA_KV_HEADS),
                   jax.ShapeDtypeStruct((B, SWA_KV_HEADS, SWA_HD + ONES_ROWS, S), BF16)],
        compiler_params=_params("parallel", "parallel"),
        name="odd_in_proj",
    )(x, w_p)


SWA_UNIT_BLOCKS = 2
SWA_UNIT_SPAN = SWA_UNIT_BLOCKS * BLOCK + 2 * WINDOW
SWA_KEY_CHUNK = 256
_SWA_SHIFTS = (-WINDOW, 0, -2 * WINDOW)


def _swa_attn_kernel(slope_ref, sink_ref, q_ref, k_ref, vt_ref, o_ref, bias_ref):
    g = pl.program_id(0)
    S = k_ref.shape[2]
    span = SWA_UNIT_SPAN
    n_units = S // (SWA_UNIT_BLOCKS * BLOCK)
    kc = SWA_KEY_CHUNK
    n_chunks = span // kc
    cols = SWA_UNIT_BLOCKS * SWA_GROUP * BLOCK
    low = lax.broadcasted_iota(jnp.int32, (BLOCK, LANES), 1) < HALF
    col = lax.broadcasted_iota(jnp.int32, (1, cols), 1)
    head_of_col = (col // BLOCK) % SWA_GROUP
    slope_row = jnp.zeros((1, cols), F32)
    sink_row = jnp.zeros((1, cols), F32)
    for hh in range(SWA_GROUP):
        slope_row = jnp.where(head_of_col == hh, slope_ref[g * SWA_GROUP + hh], slope_row)
        sink_row = jnp.where(head_of_col == hh, sink_ref[g * SWA_GROUP + hh] * LOG2E, sink_row)

    @pl.when(pl.program_id(1) == 0)
    def _():
        q_in_unit = col // (SWA_GROUP * BLOCK) * BLOCK + col % BLOCK
        rel0 = lax.broadcasted_iota(jnp.int32, (span, cols), 0) - q_in_unit
        for t, shift in enumerate(_SWA_SHIFTS):
            dist = jnp.abs(rel0 + shift)
            bias_ref[t] = jnp.where(dist <= WINDOW, -slope_row * dist.astype(F32), -jnp.inf)

    def first_query(u):
        return u * SWA_UNIT_BLOCKS * BLOCK

    def window_start(u):
        return min(max(first_query(u) - WINDOW, 0), S - span)

    def stacked_q(u):
        parts = []
        for blk in range(SWA_UNIT_BLOCKS):
            lo = first_query(u) + blk * BLOCK
            for pr in range(2):
                qp = q_ref[0, pr, lo:lo + BLOCK, :]
                zero = jnp.zeros_like(qp)
                parts += [jnp.where(low, qp, zero), jnp.where(low, zero, qp)]
        return jnp.concatenate(parts, axis=0)

    col_max = {}

    def qk_chunk(u, c):
        lo = window_start(u) + c * kc
        t = _SWA_SHIFTS.index(window_start(u) - first_query(u))
        st = _dot_nt(k_ref[0, 0, lo:lo + kc, :], stacked_q(u)) + bias_ref[t, c * kc:(c + 1) * kc, :]
        col_max[u] = jnp.maximum(col_max.get(u, sink_row), jnp.max(st, axis=0, keepdims=True))
        return st

    def pv_chunk(u, c, scores):
        lo = window_start(u) + c * kc
        return _dot(vt_ref[0, 0, :, lo:lo + kc], jnp.exp2(scores[c] - col_max[u]).astype(BF16))

    def finish(u, acc):
        m = col_max.pop(u)
        ot = acc[:SWA_HD] / (acc[SWA_HD:SWA_HD + 1] + jnp.exp2(sink_row - m))
        for blk in range(SWA_UNIT_BLOCKS):
            lo = first_query(u) + blk * BLOCK
            for pr in range(2):
                c0 = (blk * SWA_GROUP + 2 * pr) * BLOCK
                pair = jnp.concatenate([ot[:, c0:c0 + BLOCK], ot[:, c0 + BLOCK:c0 + 2 * BLOCK]], axis=0)
                o_ref[0, lo:lo + BLOCK, pr * LANES:(pr + 1) * LANES] = pair.T.astype(o_ref.dtype)

    _pipelined_units(n_units, n_chunks, qk_chunk, pv_chunk, finish)


def _swa_attention(slopes, sink, q, k, v):
    B, _, S, _ = q.shape
    smem = pl.BlockSpec(memory_space=pltpu.SMEM)
    return pl.pallas_call(
        _swa_attn_kernel,
        grid=(SWA_KV_HEADS, B),
        in_specs=[
            smem, smem,
            pl.BlockSpec((1, 2, S, LANES), lambda g, b: (b, g, 0, 0)),
            pl.BlockSpec((1, 1, S, LANES), lambda g, b: (b, g, 0, 0)),
            pl.BlockSpec((1, 1, SWA_HD + ONES_ROWS, S), lambda g, b: (b, g, 0, 0)),
        ],
        out_specs=pl.BlockSpec((1, S, 2 * LANES), lambda g, b: (b, 0, g)),
        out_shape=jax.ShapeDtypeStruct((B, S, SWA_HEADS * SWA_HD), BF16),
        scratch_shapes=[pltpu.VMEM((len(_SWA_SHIFTS), SWA_UNIT_SPAN,
                                    SWA_UNIT_BLOCKS * SWA_GROUP * BLOCK), F32)],
        compiler_params=_params("arbitrary", "arbitrary"),
        name="swa_attention",
    )(slopes, sink, q, k, v)


def _post_kernel(oa_ref, ob_ref, x_ref, woa_ref, wob_ref, g1_ref, b1_ref,
                 w1_ref, fb1_ref, w2_ref, fb2_ref, g2_ref, b2_ref, out_ref):
    hm = x_ref.shape[0] // POST_SPLIT
    rows = [slice(h * hm, (h + 1) * hm) for h in range(POST_SPLIT)]
    y = [_dot(oa_ref[r, :], woa_ref[...]) + _dot(ob_ref[r, :], wob_ref[...]) for r in rows]
    x1 = [_layer_norm(ALPHA * x_ref[r, :] + y[h], g1_ref[...], b1_ref[...]) for h, r in enumerate(rows)]
    x1b = [v.astype(BF16) for v in x1]
    acc = [jnp.zeros_like(v) + fb2_ref[...] for v in x1]
    for c in range(0, w1_ref.shape[1], FF_CHUNK):
        for h in range(POST_SPLIT):
            hc = _dot(x1b[h], w1_ref[:, c:c + FF_CHUNK]) + fb1_ref[:, c:c + FF_CHUNK]
            hc = jnp.square(jnp.maximum(hc, 0.0))
            acc[h] = acc[h] + _dot(hc.astype(BF16), w2_ref[c:c + FF_CHUNK, :])
    for h, r in enumerate(rows):
        out_ref[r, :] = _layer_norm(ALPHA * x1[h] + acc[h], g2_ref[...], b2_ref[...])


def _post(o_a, o_b, a_blk, b_blk, x, wo_a, wo_b, g1, b1, w1, fb1, w2, fb2, g2, b2):
    T, D = x.shape
    tm = min(POST_SPLIT * TOKEN_TILE, T)
    half = wo_a.shape[0]
    const = lambda i: (0, 0)
    resident = lambda a: pl.BlockSpec(a.shape, const, pipeline_mode=pl.Buffered(1))
    return pl.pallas_call(
        _post_kernel,
        grid=(T // tm,),
        in_specs=[
            pl.BlockSpec((tm, half), lambda i: (i, a_blk)),
            pl.BlockSpec((tm, half), lambda i: (i, b_blk)),
            pl.BlockSpec((tm, D), lambda i: (i, 0)),
            resident(wo_a), resident(wo_b), resident(g1), resident(b1),
            resident(w1), resident(fb1), resident(w2), resident(fb2), resident(g2), resident(b2),
        ],
        out_specs=pl.BlockSpec((tm, D), lambda i: (i, 0)),
        out_shape=jax.ShapeDtypeStruct((T, D), F32),
        compiler_params=_params("parallel"),
        name="post_mlp",
    )(o_a, o_b, x, wo_a, wo_b, g1, b1, w1, fb1, w2, fb2, g2, b2)


def _prep_even_weights(w_in, w_uq, w_ukv):
    D = w_in.shape[0]
    o_kr = MLA_Q_LORA + MLA_KV_LORA
    o_dq = o_kr + MLA_ROPE
    kr_slab = jnp.zeros((D, LANES), F32).at[:, MLA_NOPE:MLA_NOPE + MLA_ROPE].set(w_in[:, o_kr:o_dq])
    w_in_p = jnp.concatenate([w_in[:, :o_kr], kr_slab, w_in[:, o_dq:]], axis=1).astype(BF16)
    assert w_in_p.shape[1] == _EV_END

    r = w_uq.shape[0]
    qh = w_uq.reshape(r, MLA_HEADS, MLA_NOPE + MLA_ROPE)
    qh = jnp.pad(qh, ((0, 0), (0, 0), (0, LANES - MLA_NOPE - MLA_ROPE)))
    w_uq_p = qh.reshape(r, MLA_HEADS * LANES).astype(BF16)

    r = w_ukv.shape[0]
    kvh = w_ukv.reshape(r, MLA_HEADS, MLA_NOPE + MLA_V)
    kn = jnp.pad(kvh[:, :, :MLA_NOPE], ((0, 0), (0, 0), (0, LANES - MLA_NOPE)))
    vh = jnp.pad(kvh[:, :, MLA_NOPE:], ((0, 0), (0, 0), (0, LANES - MLA_V)))
    w_ukv_p = jnp.concatenate([kn.reshape(r, MLA_HEADS * LANES),
                               vh.reshape(r, MLA_HEADS * LANES)], axis=1).astype(BF16)
    return w_in_p, w_uq_p, w_ukv_p


def _prep_odd_weights(w_in):
    D = w_in.shape[0]
    nq = SWA_HEADS * SWA_HD
    nkv = SWA_KV_HEADS * SWA_HD
    dup = lambda w: jnp.concatenate([w.reshape(D, SWA_KV_HEADS, SWA_HD)] * 2, axis=2).reshape(D, 2 * nkv)
    return jnp.concatenate([w_in[:, :nq], dup(w_in[:, nq:nq + nkv]), dup(w_in[:, nq + nkv:])],
                           axis=1).astype(BF16)


def _rope_tables(S):
    half = MLA_ROPE // 2
    inv = ROPE_THETA ** (-np.arange(half, dtype=np.float32) / half)
    ang = np.arange(S, dtype=np.float32)[:, None] * inv[None, :]
    cos, sin = np.cos(ang).astype(np.float32), np.sin(ang).astype(np.float32)
    c = np.zeros((S, LANES), np.float32)
    sn = np.zeros((S, LANES), np.float32)
    sp = np.zeros((S, LANES), np.float32)
    c[:, :MLA_NOPE] = 1.0
    c[:, MLA_NOPE:MLA_NOPE + half] = cos
    c[:, MLA_NOPE + half:MLA_NOPE + 2 * half] = cos
    sn[:, MLA_NOPE:MLA_NOPE + half] = -sin
    sp[:, MLA_NOPE + half:MLA_NOPE + 2 * half] = sin
    return jnp.asarray(c), jnp.asarray(sn), jnp.asarray(sp)


def _alibi_slopes_log2(n):
    return jnp.asarray(2.0 ** (-8.0 * np.arange(1, n + 1, dtype=np.float32) / n) * LOG2E, F32)


def kernel(x, ev_w_in, ev_q_norm, ev_kv_norm, ev_w_uq, ev_w_ukv, ev_lam_q1, ev_lam_k1, ev_lam_q2, ev_lam_k2, ev_diff_norm, ev_w_out, od_w_in, od_sink, od_w_out, ln1_g, ln1_b, ln2_g, ln2_b, ffn_w1, ffn_b1, ffn_w2, ffn_b2):
    B, S, D = x.shape
    row = lambda a: a.reshape(1, -1).astype(F32)
    rope_c, rope_sn, rope_sp = _rope_tables(S)
    xs = x
    for layer in range(DEPTH):
        j = layer // 2
        if layer % 2 == 0:
            lambda_init = 0.8 - 0.6 * math.exp(-0.3 * layer)
            w_in_p, w_uq_p, w_ukv_p = _prep_even_weights(ev_w_in[j], ev_w_uq[j], ev_w_ukv[j])
            q, k, v, dq, dk, dv = _even_in_proj(xs, w_in_p, row(ev_q_norm[j]), row(ev_kv_norm[j]),
                                                w_uq_p, w_ukv_p, rope_c, rope_sn, rope_sp)
            o_mla = _mla_attention(q, k, v)
            lam_vecs = jnp.stack([ev_lam_q1[j], ev_lam_k1[j], ev_lam_q2[j], ev_lam_k2[j]]).astype(F32)
            o_diff = _diff_attention(_alibi_slopes_log2(DIFF_HEADS), lam_vecs, dq, dk, dv,
                                     row(ev_diff_norm[j]), lambda_init)
            n_a = o_mla.shape[-1]
            o_a, o_b, a_blk, b_blk = o_mla.reshape(B * S, n_a), o_diff.reshape(B * S, -1), 0, 0
            w_out = ev_w_out[j]
        else:
            q, k, v = _odd_in_proj(xs, _prep_odd_weights(od_w_in[j]))
            o = _swa_attention(_alibi_slopes_log2(SWA_HEADS), od_sink[j].astype(F32), q, k, v)
            n_a = o.shape[-1] // 2
            o_a = o_b = o.reshape(B * S, -1)
            a_blk, b_blk = 0, 1
            w_out = od_w_out[j]
        xs = _post(o_a, o_b, a_blk, b_blk, xs.reshape(B * S, D),
                   w_out[:n_a].astype(BF16), w_out[n_a:].astype(BF16),
                   row(ln1_g[layer]), row(ln1_b[layer]),
                   ffn_w1[layer].astype(BF16), row(ffn_b1[layer]),
                   ffn_w2[layer].astype(BF16), row(ffn_b2[layer]),
                   row(ln2_g[layer]), row(ln2_b[layer])).reshape(B, S, D)
    return xs
```

```python
import functools
import math

import jax
import jax.numpy as jnp
import numpy as np
from jax import lax
from jax.experimental import pallas as pl
from jax.experimental.pallas import tpu as pltpu

F32 = jnp.float32
BF16 = jnp.bfloat16

D_MODEL = 1024
DEPTH = 2
MLA_HEADS = 8
MLA_NOPE = 64
MLA_ROPE = 32
MLA_V = 64
MLA_Q_LORA = 384
MLA_KV_LORA = 256
ROPE_THETA = 10000.0
DIFF_HEADS = 4
DIFF_HD = 64
DIFF_VD = 2 * DIFF_HD
SWA_HEADS = 16
SWA_KV_HEADS = 4
SWA_HD = 64
SWA_GROUP = SWA_HEADS // SWA_KV_HEADS
WINDOW = 128
BLOCK = 128
D_FF = 4 * D_MODEL
ALPHA = (2 * DEPTH) ** 0.25
LN_EPS = 1e-5
RMS_EPS = 1e-6
LOG2E = math.log2(math.e)

LANES = 128
HALF = LANES // 2

_EV_CQ = 0
_EV_CKV = _EV_CQ + MLA_Q_LORA
_EV_KR = _EV_CKV + MLA_KV_LORA
_EV_DQ = _EV_KR + LANES
_EV_DK = _EV_DQ + DIFF_HEADS * LANES
_EV_DV = _EV_DK + DIFF_HEADS * LANES
_EV_END = _EV_DV + DIFF_HEADS * LANES

VMEM_LIMIT_BYTES = 56 * 1024 * 1024

TOKEN_TILE = 512
FULL_QUERY_TILE = 2048
ATTN_UNIT = 512
ATTN_KEY_CHUNK = 256
ONES_ROWS = 16
FF_CHUNK = 1024
POST_SPLIT = 2
IN_SPLIT = 2

BOUND_SLACK = 1.0 + 2.0 ** -6
MIN_ROW_SUM = 2.0 ** -100


def _params(*sem):
    return pltpu.CompilerParams(dimension_semantics=sem, vmem_limit_bytes=VMEM_LIMIT_BYTES)


def _dot(a, b):
    return jnp.dot(a, b, preferred_element_type=F32)


def _dot_nt(a, b):
    return lax.dot_general(a, b, (((1,), (1,)), ((), ())), preferred_element_type=F32)


def _rms(x, g):
    return x * lax.rsqrt(jnp.mean(x * x, axis=-1, keepdims=True) + RMS_EPS) * g


def _layer_norm(x, g, b):
    mu = jnp.mean(x, axis=-1, keepdims=True)
    xc = x - mu
    var = jnp.mean(xc * xc, axis=-1, keepdims=True)
    return xc * lax.rsqrt(var + LN_EPS) * g + b


def _rope_slab(blk, c, s_next, s_prev):
    nxt = pltpu.roll(blk, LANES - MLA_ROPE // 2, 1)
    prv = pltpu.roll(blk, MLA_ROPE // 2, 1)
    return blk * c + nxt * s_next + prv * s_prev


def _sq_norm(x):
    xf = x.astype(F32)
    return jnp.sum(xf * xf, axis=1, keepdims=True)


def _max_sq_norm_tile(x):
    return jnp.broadcast_to(jnp.max(_sq_norm(x), axis=0, keepdims=True), (8, LANES))


def _row_sq_norms(x):
    xf = x.astype(F32)
    sq = xf * xf
    hi = sq.astype(BF16)
    lo = (sq - hi.astype(F32)).astype(BF16)
    ones = jnp.ones((8, x.shape[1]), BF16)
    return (_dot_nt(ones, hi) + _dot_nt(ones, lo))[0:1]


def _pipelined_units(n_units, n_chunks, qk_chunk, pv_chunk, finish=None):
    results = []
    cur = None
    for stage in range(n_units + 1):
        nxt = [] if stage < n_units else None
        acc = None
        for c in range(n_chunks):
            if nxt is not None:
                nxt.append(qk_chunk(stage, c))
            if cur is not None:
                d = pv_chunk(stage - 1, c, cur)
                acc = d if acc is None else acc + d
        if cur is not None:
            if finish is None:
                results.append(acc)
            else:
                finish(stage - 1, acc)
        cur = nxt
    return results


def _staged(n, front, back):
    carried = None
    for i in range(n + 1):
        nxt = front(i) if i < n else None
        if carried is not None:
            back(i - 1, carried)
        carried = nxt


def _all_rows_healthy(row_sums):
    return jnp.min(jnp.where(row_sums >= MIN_ROW_SUM, 1.0, 0.0)) > 0.5


def _even_in_kernel(x_ref, w_in_ref, qn_ref, kvn_ref, w_uq_ref, w_ukv_ref, c_ref, sn_ref, sp_ref,
                    q_ref, k_ref, v_ref, dq_ref, dk_ref, dv_ref):
    hm = x_ref.shape[1] // IN_SPLIT
    low = lax.broadcasted_iota(jnp.int32, (hm, LANES), 1) < HALF
    ones = jnp.ones((ONES_ROWS, hm), BF16)
    q_scale = (MLA_NOPE + MLA_ROPE) ** -0.5 * LOG2E
    dq_scale = DIFF_HD ** -0.5 * LOG2E

    def front(i):
        h = _dot(x_ref[0, i * hm:(i + 1) * hm, :].astype(BF16), w_in_ref[...])
        cq = _rms(h[:, _EV_CQ:_EV_CKV], qn_ref[...])
        q = _dot(cq.astype(BF16), w_uq_ref[...])
        ckv = _rms(h[:, _EV_CKV:_EV_KR], kvn_ref[...])
        kv = _dot(ckv.astype(BF16), w_ukv_ref[...])
        return h, q, kv

    def back(i, vals):
        h, q, kv = vals
        r = slice(i * hm, (i + 1) * hm)
        c, sn, sp = c_ref[r, :], sn_ref[r, :], sp_ref[r, :]
        for hd in range(MLA_HEADS):
            blk = q[:, hd * LANES:(hd + 1) * LANES]
            q_ref[0, hd, r, :] = (_rope_slab(blk, c, sn, sp) * q_scale).astype(BF16)
        kr = _rope_slab(h[:, _EV_KR:_EV_DQ], c, sn, sp)
        for hd in range(MLA_HEADS):
            slab = kv[:, hd * LANES:(hd + 1) * LANES]
            k_ref[0, hd, r, :] = (jnp.where(low, slab, 0.0) + kr).astype(BF16)
            v_ref[0, hd, :, r] = jnp.where(low, 1.0, slab).T.astype(BF16)
        for hd in range(DIFF_HEADS):
            dq_ref[0, hd, r, :] = (h[:, _EV_DQ + hd * LANES:_EV_DQ + (hd + 1) * LANES] * dq_scale).astype(BF16)
            dk_ref[0, hd, r, :] = h[:, _EV_DK + hd * LANES:_EV_DK + (hd + 1) * LANES].astype(BF16)
            dv_ref[0, hd, :LANES, r] = h[:, _EV_DV + hd * LANES:_EV_DV + (hd + 1) * LANES].T.astype(BF16)
            dv_ref[0, hd, LANES:, r] = ones

    _staged(IN_SPLIT, front, back)


def _even_in_proj(x, w_in_p, q_norm, kv_norm, w_uq_p, w_ukv_p, rope_c, rope_sn, rope_sp):
    B, S, D = x.shape
    tm = min(IN_SPLIT * TOKEN_TILE, S)
    nt = S // tm
    const = lambda b, i: (0, 0)
    head_out = lambda n, w=LANES: pl.BlockSpec((1, n, tm, w), lambda b, i: (b, 0, i, 0))
    head_shape = lambda n, w=LANES: jax.ShapeDtypeStruct((B, n, S, w), BF16)
    head_out_t = lambda n, r: pl.BlockSpec((1, n, r, tm), lambda b, i: (b, 0, 0, i))
    head_shape_t = lambda n, r: jax.ShapeDtypeStruct((B, n, r, S), BF16)
    return pl.pallas_call(
        _even_in_kernel,
        grid=(B, nt),
        in_specs=[
            pl.BlockSpec((1, tm, D), lambda b, i: (b, i, 0)),
            pl.BlockSpec(w_in_p.shape, const),
            pl.BlockSpec(q_norm.shape, const),
            pl.BlockSpec(kv_norm.shape, const),
            pl.BlockSpec(w_uq_p.shape, const),
            pl.BlockSpec(w_ukv_p.shape, const),
            pl.BlockSpec((tm, LANES), lambda b, i: (i, 0)),
            pl.BlockSpec((tm, LANES), lambda b, i: (i, 0)),
            pl.BlockSpec((tm, LANES), lambda b, i: (i, 0)),
        ],
        out_specs=[head_out(MLA_HEADS), head_out(MLA_HEADS), head_out_t(MLA_HEADS, LANES),
                   head_out(DIFF_HEADS), head_out(DIFF_HEADS), head_out_t(DIFF_HEADS, LANES + ONES_ROWS)],
        out_shape=[head_shape(MLA_HEADS), head_shape(MLA_HEADS), head_shape_t(MLA_HEADS, LANES),
                   head_shape(DIFF_HEADS), head_shape(DIFF_HEADS),
                   head_shape_t(DIFF_HEADS, LANES + ONES_ROWS)],
        compiler_params=_params("parallel", "parallel"),
        name="even_in_proj",
    )(x, w_in_p, q_norm, kv_norm, w_uq_p, w_ukv_p, rope_c, rope_sn, rope_sp)


def _mla_attn_kernel(q_ref, k_ref, vt_ref, o_ref, kmax_ref):
    S, tq = k_ref.shape[2], q_ref.shape[2]
    unit, kc = min(ATTN_UNIT, tq), min(ATTN_KEY_CHUNK, S)

    @pl.when(pl.program_id(2) == 0)
    def _():
        for hh in range(2):
            kmax_ref[hh] = _max_sq_norm_tile(k_ref[0, hh])

    units = [(hh, u) for u in range(tq // unit) for hh in range(2)]

    def q_of(i):
        hh, u = units[i]
        return q_ref[0, hh, u * unit:(u + 1) * unit, :]

    bounds = [jnp.sqrt(_row_sq_norms(q_of(i)) * kmax_ref[units[i][0]][0:1, 0:1]) * BOUND_SLACK
              for i in range(len(units))]

    def qk_chunk(i, c):
        return _dot_nt(k_ref[0, units[i][0], c * kc:(c + 1) * kc, :], q_of(i))

    def pv_chunk(i, c, scores):
        pt = jnp.exp2(scores[c] - bounds[i]).astype(BF16)
        return _dot(vt_ref[0, units[i][0], :, c * kc:(c + 1) * kc], pt)

    def write(accs):
        for u in range(tq // unit):
            ot = jnp.concatenate([a[HALF:] / a[:HALF] for a in accs[2 * u:2 * u + 2]], axis=0)
            o_ref[0, u * unit:(u + 1) * unit, :] = ot.T.astype(o_ref.dtype)

    accs = _pipelined_units(len(units), S // kc, qk_chunk, pv_chunk)
    write(accs)
    sums = accs[0][:8]
    for a in accs[1:]:
        sums = jnp.minimum(sums, a[:8])
    healthy = _all_rows_healthy(sums)

    @pl.when(jnp.logical_not(healthy))
    def _():
        exact = []
        for i, (hh, _) in enumerate(units):
            st = _dot_nt(k_ref[0, hh], q_of(i))
            pt = jnp.exp2(st - jnp.max(st, axis=0, keepdims=True)).astype(BF16)
            exact.append(_dot(vt_ref[0, hh], pt))
        write(exact)


def _mla_attention(q, k, v):
    B, H, S, _ = q.shape
    tq = min(FULL_QUERY_TILE, S)
    return pl.pallas_call(
        _mla_attn_kernel,
        grid=(B, H // 2, S // tq),
        in_specs=[
            pl.BlockSpec((1, 2, tq, LANES), lambda b, j, i: (b, j, i, 0)),
            pl.BlockSpec((1, 2, S, LANES), lambda b, j, i: (b, j, 0, 0)),
            pl.BlockSpec((1, 2, LANES, S), lambda b, j, i: (b, j, 0, 0)),
        ],
        out_specs=pl.BlockSpec((1, tq, LANES), lambda b, j, i: (b, i, j)),
        out_shape=jax.ShapeDtypeStruct((B, S, H * MLA_V), BF16),
        scratch_shapes=[pltpu.VMEM((2, 8, LANES), F32)],
        compiler_params=_params("parallel", "parallel", "arbitrary"),
        name="mla_attention",
    )(q, k, v)


def _diff_attn_kernel(slope_ref, lam_ref, q_ref, k_ref, vt_ref, g_ref, o_ref, kmax_ref, *, lambda_init):
    hd = pl.program_id(1)
    qi = pl.program_id(2)
    S, tq = k_ref.shape[2], q_ref.shape[2]
    unit, kc = min(ATTN_UNIT, tq), min(ATTN_KEY_CHUNK, S)
    slope = slope_ref[hd]

    @pl.when(qi == 0)
    def _():
        k = k_ref[0, 0]
        k_lane = lax.broadcasted_iota(jnp.int32, k.shape, 1)
        k_zero = jnp.zeros_like(k)
        kmax_ref[0] = _max_sq_norm_tile(jnp.where(k_lane < HALF, k, k_zero))
        kmax_ref[1] = _max_sq_norm_tile(jnp.where(k_lane >= HALF, k, k_zero))

    lv = lam_ref[...]
    lam = (jnp.exp(jnp.sum(lv[0:1] * lv[1:2], axis=-1, keepdims=True))
           - jnp.exp(jnp.sum(lv[2:3] * lv[3:4], axis=-1, keepdims=True)) + lambda_init)

    units = [(mp, u) for u in range(tq // unit) for mp in range(2)]
    lane = lax.broadcasted_iota(jnp.int32, (unit, LANES), 1)

    def q_of(i):
        mp, u = units[i]
        q = q_ref[0, 0, u * unit:(u + 1) * unit, :]
        return jnp.where((lane < HALF) if mp == 0 else (lane >= HALF), q, jnp.zeros_like(q))

    def q_pos(i):
        u = units[i][1]
        return (qi * tq + u * unit + lax.broadcasted_iota(jnp.int32, (1, unit), 1)).astype(F32) * slope

    def k_pos(c, n):
        return (c + lax.broadcasted_iota(jnp.int32, (n, 1), 0)).astype(F32) * slope

    bounds = [jnp.sqrt(_row_sq_norms(q_of(i)) * kmax_ref[units[i][0]][0:1, 0:1]) * BOUND_SLACK
              for i in range(len(units))]

    def qk_chunk(i, c):
        return _dot_nt(k_ref[0, 0, c * kc:(c + 1) * kc, :], q_of(i))

    def pv_chunk(i, c, scores):
        dist = jnp.abs(k_pos(c * kc, kc) - q_pos(i))
        pt = jnp.exp2(scores[c] - dist - bounds[i]).astype(BF16)
        return _dot(vt_ref[0, 0, :, c * kc:(c + 1) * kc], pt)

    def write(accs):
        for u in range(tq // unit):
            outs = [a[:LANES] / a[LANES:LANES + 1] for a in accs[2 * u:2 * u + 2]]
            o = (outs[0] - lam * outs[1]).T
            o = _rms(o, g_ref[...]) * (1.0 - lambda_init)
            o_ref[0, u * unit:(u + 1) * unit, :] = o.astype(o_ref.dtype)

    accs = _pipelined_units(len(units), S // kc, qk_chunk, pv_chunk)
    write(accs)
    sums = accs[0][LANES:LANES + 8]
    for a in accs[1:]:
        sums = jnp.minimum(sums, a[LANES:LANES + 8])
    healthy = _all_rows_healthy(sums)

    @pl.when(jnp.logical_not(healthy))
    def _():
        exact = []
        for i in range(len(units)):
            st = _dot_nt(k_ref[0, 0], q_of(i)) - jnp.abs(k_pos(0, S) - q_pos(i))
            pt = jnp.exp2(st - jnp.max(st, axis=0, keepdims=True)).astype(BF16)
            exact.append(_dot(vt_ref[0, 0], pt))
        write(exact)


def _diff_attention(slopes, lam_vecs, q, k, v, diff_norm, lambda_init):
    B, H, S, _ = q.shape
    tq = min(FULL_QUERY_TILE, S)
    smem = pl.BlockSpec(memory_space=pltpu.SMEM)
    return pl.pallas_call(
        functools.partial(_diff_attn_kernel, lambda_init=lambda_init),
        grid=(B, H, S // tq),
        in_specs=[
            smem,
            pl.BlockSpec(lam_vecs.shape, lambda b, h, i: (0, 0)),
            pl.BlockSpec((1, 1, tq, LANES), lambda b, h, i: (b, h, i, 0)),
            pl.BlockSpec((1, 1, S, LANES), lambda b, h, i: (b, h, 0, 0)),
            pl.BlockSpec((1, 1, LANES + ONES_ROWS, S), lambda b, h, i: (b, h, 0, 0)),
            pl.BlockSpec(diff_norm.shape, lambda b, h, i: (0, 0)),
        ],
        out_specs=pl.BlockSpec((1, tq, LANES), lambda b, h, i: (b, i, h)),
        out_shape=jax.ShapeDtypeStruct((B, S, H * DIFF_VD), BF16),
        scratch_shapes=[pltpu.VMEM((2, 8, LANES), F32)],
        compiler_params=_params("parallel", "parallel", "arbitrary"),
        name="diff_attention",
    )(slopes, lam_vecs, q, k, v, diff_norm)


def _odd_in_kernel(x_ref, w_ref, q_ref, k_ref, v_ref):
    hm = x_ref.shape[1] // IN_SPLIT
    nq = SWA_HEADS // 2
    nkv = SWA_KV_HEADS // 2
    q_scale = SWA_HD ** -0.5 * LOG2E
    low = lax.broadcasted_iota(jnp.int32, (hm, LANES), 1) < HALF
    ones = jnp.ones((ONES_ROWS, hm), BF16)

    def front(i):
        return _dot(x_ref[0, i * hm:(i + 1) * hm, :].astype(BF16), w_ref[...])

    def back(i, h):
        r = slice(i * hm, (i + 1) * hm)
        for pr in range(nq):
            q_ref[0, pr, r, :] = (h[:, pr * LANES:(pr + 1) * LANES] * q_scale).astype(BF16)
        for j in range(nkv):
            kp = h[:, (nq + j) * LANES:(nq + j + 1) * LANES]
            swapped = pltpu.roll(kp, HALF, 1)
            k_ref[0, 2 * j, r, :] = jnp.where(low, kp, swapped).astype(BF16)
            k_ref[0, 2 * j + 1, r, :] = jnp.where(low, swapped, kp).astype(BF16)
            vt = h[:, (nq + nkv + j) * LANES:(nq + nkv + j + 1) * LANES].T.astype(BF16)
            for half in range(2):
                v_ref[0, 2 * j + half, :SWA_HD, r] = vt[half * SWA_HD:(half + 1) * SWA_HD]
                v_ref[0, 2 * j + half, SWA_HD:, r] = ones

    _staged(IN_SPLIT, front, back)


def _odd_in_proj(x, w_p):
    B, S, D = x.shape
    tm = min(IN_SPLIT * TOKEN_TILE, S)
    head_out = lambda n, w=LANES: pl.BlockSpec((1, n, tm, w), lambda b, i: (b, 0, i, 0))
    head_shape = lambda n, w=LANES: jax.ShapeDtypeStruct((B, n, S, w), BF16)
    return pl.pallas_call(
        _odd_in_kernel,
        grid=(B, S // tm),
        in_specs=[
            pl.BlockSpec((1, tm, D), lambda b, i: (b, i, 0)),
            pl.BlockSpec(w_p.shape, lambda b, i: (0, 0)),
        ],
        out_specs=[head_out(SWA_HEADS // 2), head_out(SWA_KV_HEADS),
                   pl.BlockSpec((1, SWA_KV_HEADS, SWA_HD + ONES_ROWS, tm), lambda b, i: (b, 0, 0, i))],
        out_shape=[head_shape(SWA_HEADS // 2), head_shape(SWA_KV_HEADS),
                   jax.ShapeDtypeStruct((B, SWA_KV_HEADS, SWA_HD + ONES_ROWS, S), BF16)],
        compiler_params=_params("parallel", "parallel"),
        name="odd_in_proj",
    )(x, w_p)


SWA_UNIT_BLOCKS = 2
SWA_UNIT_SPAN = SWA_UNIT_BLOCKS * BLOCK + 2 * WINDOW
SWA_KEY_CHUNK = 256
_SWA_SHIFTS = (-WINDOW, 0, -2 * WINDOW)


def _swa_attn_kernel(slope_ref, sink_ref, q_ref, k_ref, vt_ref, o_ref, bias_ref):
    g = pl.program_id(0)
    S = k_ref.shape[2]
    span = SWA_UNIT_SPAN
    n_units = S // (SWA_UNIT_BLOCKS * BLOCK)
    kc = SWA_KEY_CHUNK
    n_chunks = span // kc
    cols = SWA_UNIT_BLOCKS * SWA_GROUP * BLOCK
    low = lax.broadcasted_iota(jnp.int32, (BLOCK, LANES), 1) < HALF
    col = lax.broadcasted_iota(jnp.int32, (1, cols), 1)
    head_of_col = (col // BLOCK) % SWA_GROUP
    slope_row = jnp.zeros((1, cols), F32)
    sink_row = jnp.zeros((1, cols), F32)
    for hh in range(SWA_GROUP):
        slope_row = jnp.where(head_of_col == hh, slope_ref[g * SWA_GROUP + hh], slope_row)
        sink_row = jnp.where(head_of_col == hh, sink_ref[g * SWA_GROUP + hh] * LOG2E, sink_row)

    @pl.when(pl.program_id(1) == 0)
    def _():
        q_in_unit = col // (SWA_GROUP * BLOCK) * BLOCK + col % BLOCK
        rel0 = lax.broadcasted_iota(jnp.int32, (span, cols), 0) - q_in_unit
        for t, shift in enumerate(_SWA_SHIFTS):
            dist = jnp.abs(rel0 + shift)
            bias_ref[t] = jnp.where(dist <= WINDOW, -slope_row * dist.astype(F32), -jnp.inf)

    def first_query(u):
        return u * SWA_UNIT_BLOCKS * BLOCK

    def window_start(u):
        return min(max(first_query(u) - WINDOW, 0), S - span)

    def stacked_q(u):
        parts = []
        for blk in range(SWA_UNIT_BLOCKS):
            lo = first_query(u) + blk * BLOCK
            for pr in range(2):
                qp = q_ref[0, pr, lo:lo + BLOCK, :]
                zero = jnp.zeros_like(qp)
                parts += [jnp.where(low, qp, zero), jnp.where(low, zero, qp)]
        return jnp.concatenate(parts, axis=0)

    col_max = {}

    def qk_chunk(u, c):
        lo = window_start(u) + c * kc
        t = _SWA_SHIFTS.index(window_start(u) - first_query(u))
        st = _dot_nt(k_ref[0, 0, lo:lo + kc, :], stacked_q(u)) + bias_ref[t, c * kc:(c + 1) * kc, :]
        col_max[u] = jnp.maximum(col_max.get(u, sink_row), jnp.max(st, axis=0, keepdims=True))
        return st

    def pv_chunk(u, c, scores):
        lo = window_start(u) + c * kc
        return _dot(vt_ref[0, 0, :, lo:lo + kc], jnp.exp2(scores[c] - col_max[u]).astype(BF16))

    def finish(u, acc):
        m = col_max.pop(u)
        ot = acc[:SWA_HD] / (acc[SWA_HD:SWA_HD + 1] + jnp.exp2(sink_row - m))
        for blk in range(SWA_UNIT_BLOCKS):
            lo = first_query(u) + blk * BLOCK
            for pr in range(2):
                c0 = (blk * SWA_GROUP + 2 * pr) * BLOCK
                pair = jnp.concatenate([ot[:, c0:c0 + BLOCK], ot[:, c0 + BLOCK:c0 + 2 * BLOCK]], axis=0)
                o_ref[0, lo:lo + BLOCK, pr * LANES:(pr + 1) * LANES] = pair.T.astype(o_ref.dtype)

    _pipelined_units(n_units, n_chunks, qk_chunk, pv_chunk, finish)


def _swa_attention(slopes, sink, q, k, v):
    B, _, S, _ = q.shape
    smem = pl.BlockSpec(memory_space=pltpu.SMEM)
    return pl.pallas_call(
        _swa_attn_kernel,
        grid=(SWA_KV_HEADS, B),
        in_specs=[
            smem, smem,
            pl.BlockSpec((1, 2, S, LANES), lambda g, b: (b, g, 0, 0)),
            pl.BlockSpec((1, 1, S, LANES), lambda g, b: (b, g, 0, 0)),
            pl.BlockSpec((1, 1, SWA_HD + ONES_ROWS, S), lambda g, b: (b, g, 0, 0)),
        ],
        out_specs=pl.BlockSpec((1, S, 2 * LANES), lambda g, b: (b, 0, g)),
        out_shape=jax.ShapeDtypeStruct((B, S, SWA_HEADS * SWA_HD), BF16),
        scratch_shapes=[pltpu.VMEM((len(_SWA_SHIFTS), SWA_UNIT_SPAN,
                                    SWA_UNIT_BLOCKS * SWA_GROUP * BLOCK), F32)],
        compiler_params=_params("arbitrary", "arbitrary"),
        name="swa_attention",
    )(slopes, sink, q, k, v)


def _post_kernel(oa_ref, ob_ref, x_ref, woa_ref, wob_ref, g1_ref, b1_ref,
                 w1_ref, fb1_ref, w2_ref, fb2_ref, g2_ref, b2_ref, out_ref):
    hm = x_ref.shape[0] // POST_SPLIT
    rows = [slice(h * hm, (h + 1) * hm) for h in range(POST_SPLIT)]
    y = [_dot(oa_ref[r, :], woa_ref[...]) + _dot(ob_ref[r, :], wob_ref[...]) for r in rows]
    x1 = [_layer_norm(ALPHA * x_ref[r, :] + y[h], g1_ref[...], b1_ref[...]) for h, r in enumerate(rows)]
    x1b = [v.astype(BF16) for v in x1]
    acc = [jnp.zeros_like(v) + fb2_ref[...] for v in x1]
    for c in range(0, w1_ref.shape[1], FF_CHUNK):
        for h in range(POST_SPLIT):
            hc = _dot(x1b[h], w1_ref[:, c:c + FF_CHUNK]) + fb1_ref[:, c:c + FF_CHUNK]
            hc = jnp.square(jnp.maximum(hc, 0.0))
            acc[h] = acc[h] + _dot(hc.astype(BF16), w2_ref[c:c + FF_CHUNK, :])
    for h, r in enumerate(rows):
        out_ref[r, :] = _layer_norm(ALPHA * x1[h] + acc[h], g2_ref[...], b2_ref[...])


def _post(o_a, o_b, a_blk, b_blk, x, wo_a, wo_b, g1, b1, w1, fb1, w2, fb2, g2, b2):
    T, D = x.shape
    tm = min(POST_SPLIT * TOKEN_TILE, T)
    half = wo_a.shape[0]
    const = lambda i: (0, 0)
    resident = lambda a: pl.BlockSpec(a.shape, const, pipeline_mode=pl.Buffered(1))
    return pl.pallas_call(
        _post_kernel,
        grid=(T // tm,),
        in_specs=[
            pl.BlockSpec((tm, half), lambda i: (i, a_blk)),
            pl.BlockSpec((tm, half), lambda i: (i, b_blk)),
            pl.BlockSpec((tm, D), lambda i: (i, 0)),
            resident(wo_a), resident(wo_b), resident(g1), resident(b1),
            resident(w1), resident(fb1), resident(w2), resident(fb2), resident(g2), resident(b2),
        ],
        out_specs=pl.BlockSpec((tm, D), lambda i: (i, 0)),
        out_shape=jax.ShapeDtypeStruct((T, D), F32),
        compiler_params=_params("parallel"),
        name="post_mlp",
    )(o_a, o_b, x, wo_a, wo_b, g1, b1, w1, fb1, w2, fb2, g2, b2)


def _prep_even_weights(w_in, w_uq, w_ukv):
    D = w_in.shape[0]
    o_kr = MLA_Q_LORA + MLA_KV_LORA
    o_dq = o_kr + MLA_ROPE
    kr_slab = jnp.zeros((D, LANES), F32).at[:, MLA_NOPE:MLA_NOPE + MLA_ROPE].set(w_in[:, o_kr:o_dq])
    w_in_p = jnp.concatenate([w_in[:, :o_kr], kr_slab, w_in[:, o_dq:]], axis=1).astype(BF16)
    assert w_in_p.shape[1] == _EV_END

    r = w_uq.shape[0]
    qh = w_uq.reshape(r, MLA_HEADS, MLA_NOPE + MLA_ROPE)
    qh = jnp.pad(qh, ((0, 0), (0, 0), (0, LANES - MLA_NOPE - MLA_ROPE)))
    w_uq_p = qh.reshape(r, MLA_HEADS * LANES).astype(BF16)

    assert MLA_NOPE + MLA_V == LANES
    return w_in_p, w_uq_p, w_ukv.astype(BF16)


def _rope_tables(S):
    half = MLA_ROPE // 2
    inv = ROPE_THETA ** (-np.arange(half, dtype=np.float32) / half)
    ang = np.arange(S, dtype=np.float32)[:, None] * inv[None, :]
    cos, sin = np.cos(ang).astype(np.float32), np.sin(ang).astype(np.float32)
    c = np.zeros((S, LANES), np.float32)
    sn = np.zeros((S, LANES), np.float32)
    sp = np.zeros((S, LANES), np.float32)
    c[:, :MLA_NOPE] = 1.0
    c[:, MLA_NOPE:MLA_NOPE + half] = cos
    c[:, MLA_NOPE + half:MLA_NOPE + 2 * half] = cos
    sn[:, MLA_NOPE:MLA_NOPE + half] = -sin
    sp[:, MLA_NOPE + half:MLA_NOPE + 2 * half] = sin
    return jnp.asarray(c), jnp.asarray(sn), jnp.asarray(sp)


def _alibi_slopes_log2(n):
    return jnp.asarray(2.0 ** (-8.0 * np.arange(1, n + 1, dtype=np.float32) / n) * LOG2E, F32)


def kernel(x, ev_w_in, ev_q_norm, ev_kv_norm, ev_w_uq, ev_w_ukv, ev_lam_q1, ev_lam_k1, ev_lam_q2, ev_lam_k2, ev_diff_norm, ev_w_out, od_w_in, od_sink, od_w_out, ln1_g, ln1_b, ln2_g, ln2_b, ffn_w1, ffn_b1, ffn_w2, ffn_b2):
    B, S, D = x.shape
    row = lambda a: a.reshape(1, -1).astype(F32)
    rope_c, rope_sn, rope_sp = _rope_tables(S)
    xs = x
    for layer in range(DEPTH):
        j = layer // 2
        if layer % 2 == 0:
            lambda_init = 0.8 - 0.6 * math.exp(-0.3 * layer)
            w_in_p, w_uq_p, w_ukv_p = _prep_even_weights(ev_w_in[j], ev_w_uq[j], ev_w_ukv[j])
            q, k, v, dq, dk, dv = _even_in_proj(xs, w_in_p, row(ev_q_norm[j]), row(ev_kv_norm[j]),
                                                w_uq_p, w_ukv_p, rope_c, rope_sn, rope_sp)
            o_mla = _mla_attention(q, k, v)
            lam_vecs = jnp.stack([ev_lam_q1[j], ev_lam_k1[j], ev_lam_q2[j], ev_lam_k2[j]]).astype(F32)
            o_diff = _diff_attention(_alibi_slopes_log2(DIFF_HEADS), lam_vecs, dq, dk, dv,
                                     row(ev_diff_norm[j]), lambda_init)
            n_a = o_mla.shape[-1]
            o_a, o_b, a_blk, b_blk = o_mla.reshape(B * S, n_a), o_diff.reshape(B * S, -1), 0, 0
            w_out = ev_w_out[j]
        else:
            q, k, v = _odd_in_proj(xs, od_w_in[j].astype(BF16))
            o = _swa_attention(_alibi_slopes_log2(SWA_HEADS), od_sink[j].astype(F32), q, k, v)
            n_a = o.shape[-1] // 2
            o_a = o_b = o.reshape(B * S, -1)
            a_blk, b_blk = 0, 1
            w_out = od_w_out[j]
        xs = _post(o_a, o_b, a_blk, b_blk, xs.reshape(B * S, D),
                   w_out[:n_a].astype(BF16), w_out[n_a:].astype(BF16),
                   row(ln1_g[layer]), row(ln1_b[layer]),
                   ffn_w1[layer].astype(BF16), row(ffn_b1[layer]),
                   ffn_w2[layer].astype(BF16), row(ffn_b2[layer]),
                   row(ln2_g[layer]), row(ln2_b[layer])).reshape(B, S, D)
    return xs
```

```python
import functools
import math

import jax
import jax.numpy as jnp
import numpy as np
from jax import lax
from jax.experimental import pallas as pl
from jax.experimental.pallas import tpu as pltpu

F32 = jnp.float32
BF16 = jnp.bfloat16

D_MODEL = 1024
DEPTH = 2
MLA_HEADS = 8
MLA_NOPE = 64
MLA_ROPE = 32
MLA_V = 64
MLA_Q_LORA = 384
MLA_KV_LORA = 256
ROPE_THETA = 10000.0
DIFF_HEADS = 4
DIFF_HD = 64
DIFF_VD = 2 * DIFF_HD
SWA_HEADS = 16
SWA_KV_HEADS = 4
SWA_HD = 64
SWA_GROUP = SWA_HEADS // SWA_KV_HEADS
WINDOW = 128
BLOCK = 128
D_FF = 4 * D_MODEL
ALPHA = (2 * DEPTH) ** 0.25
LN_EPS = 1e-5
RMS_EPS = 1e-6
LOG2E = math.log2(math.e)

LANES = 128
HALF = LANES // 2

_EV_CQ = 0
_EV_CKV = _EV_CQ + MLA_Q_LORA
_EV_KR = _EV_CKV + MLA_KV_LORA
_EV_DQ = _EV_KR + LANES
_EV_DK = _EV_DQ + DIFF_HEADS * LANES
_EV_DV = _EV_DK + DIFF_HEADS * LANES
_EV_END = _EV_DV + DIFF_HEADS * LANES

VMEM_LIMIT_BYTES = 56 * 1024 * 1024

TOKEN_TILE = 512
POST_TILE = 256
FULL_QUERY_TILE = 2048
ATTN_UNIT = 512
ATTN_KEY_CHUNK = 256
ONES_ROWS = 16
FF_CHUNK = 1024
POST_SPLIT = 4
IN_SPLIT = 2

BOUND_SLACK = 1.0 + 2.0 ** -6
MIN_ROW_SUM = 2.0 ** -100


def _params(*sem):
    return pltpu.CompilerParams(dimension_semantics=sem, vmem_limit_bytes=VMEM_LIMIT_BYTES)


def _dot(a, b):
    return jnp.dot(a, b, preferred_element_type=F32)


def _dot_nt(a, b):
    return lax.dot_general(a, b, (((1,), (1,)), ((), ())), preferred_element_type=F32)


def _rms(x, g):
    return x * lax.rsqrt(jnp.mean(x * x, axis=-1, keepdims=True) + RMS_EPS) * g


def _layer_norm(x, g, b):
    mu = jnp.mean(x, axis=-1, keepdims=True)
    xc = x - mu
    var = jnp.mean(xc * xc, axis=-1, keepdims=True)
    return xc * lax.rsqrt(var + LN_EPS) * g + b


def _rope_slab(blk, c, s_next, s_prev):
    nxt = pltpu.roll(blk, LANES - MLA_ROPE // 2, 1)
    prv = pltpu.roll(blk, MLA_ROPE // 2, 1)
    return blk * c + nxt * s_next + prv * s_prev


def _sq_norm(x):
    xf = x.astype(F32)
    return jnp.sum(xf * xf, axis=1, keepdims=True)


def _max_sq_norm_tile(x):
    return jnp.broadcast_to(jnp.max(_sq_norm(x), axis=0, keepdims=True), (8, LANES))


def _row_sq_norms(x):
    xf = x.astype(F32)
    sq = xf * xf
    hi = sq.astype(BF16)
    lo = (sq - hi.astype(F32)).astype(BF16)
    ones = jnp.ones((8, x.shape[1]), BF16)
    return (_dot_nt(ones, hi) + _dot_nt(ones, lo))[0:1]


def _pipelined_units(n_units, n_chunks, qk_chunk, pv_chunk, finish=None):
    results = []
    cur = None
    for stage in range(n_units + 1):
        nxt = [] if stage < n_units else None
        acc = None
        for c in range(n_chunks):
            if nxt is not None:
                nxt.append(qk_chunk(stage, c))
            if cur is not None:
                d = pv_chunk(stage - 1, c, cur)
                acc = d if acc is None else acc + d
        if cur is not None:
            if finish is None:
                results.append(acc)
            else:
                finish(stage - 1, acc)
        cur = nxt
    return results


def _staged(n, front, back):
    carried = None
    for i in range(n + 1):
        nxt = front(i) if i < n else None
        if carried is not None:
            back(i - 1, carried)
        carried = nxt


def _all_rows_healthy(row_sums):
    return jnp.min(jnp.where(row_sums >= MIN_ROW_SUM, 1.0, 0.0)) > 0.5


def _even_in_kernel(x_ref, w_in_ref, qn_ref, kvn_ref, w_uq_ref, w_ukv_ref, c_ref, sn_ref, sp_ref,
                    q_ref, k_ref, v_ref, dq_ref, dk_ref, dv_ref):
    hm = x_ref.shape[1] // IN_SPLIT
    low = lax.broadcasted_iota(jnp.int32, (hm, LANES), 1) < HALF
    ones = jnp.ones((ONES_ROWS, hm), BF16)
    q_scale = (MLA_NOPE + MLA_ROPE) ** -0.5 * LOG2E
    dq_scale = DIFF_HD ** -0.5 * LOG2E

    def front(i):
        h = _dot(x_ref[0, i * hm:(i + 1) * hm, :].astype(BF16), w_in_ref[...])
        cq = _rms(h[:, _EV_CQ:_EV_CKV], qn_ref[...])
        q = _dot(cq.astype(BF16), w_uq_ref[...])
        ckv = _rms(h[:, _EV_CKV:_EV_KR], kvn_ref[...])
        kv = _dot(ckv.astype(BF16), w_ukv_ref[...])
        return h, q, kv

    def back(i, vals):
        h, q, kv = vals
        r = slice(i * hm, (i + 1) * hm)
        c, sn, sp = c_ref[r, :], sn_ref[r, :], sp_ref[r, :]
        for hd in range(MLA_HEADS):
            blk = q[:, hd * LANES:(hd + 1) * LANES]
            q_ref[0, hd, r, :] = (_rope_slab(blk, c, sn, sp) * q_scale).astype(BF16)
        kr = _rope_slab(h[:, _EV_KR:_EV_DQ], c, sn, sp)
        for hd in range(MLA_HEADS):
            slab = kv[:, hd * LANES:(hd + 1) * LANES]
            k_ref[0, hd, r, :] = (jnp.where(low, slab, 0.0) + kr).astype(BF16)
            v_ref[0, hd, :, r] = jnp.where(low, 1.0, slab).T.astype(BF16)
        for hd in range(DIFF_HEADS):
            dq_ref[0, hd, r, :] = (h[:, _EV_DQ + hd * LANES:_EV_DQ + (hd + 1) * LANES] * dq_scale).astype(BF16)
            dk_ref[0, hd, r, :] = h[:, _EV_DK + hd * LANES:_EV_DK + (hd + 1) * LANES].astype(BF16)
            dv_ref[0, hd, :LANES, r] = h[:, _EV_DV + hd * LANES:_EV_DV + (hd + 1) * LANES].T.astype(BF16)
            dv_ref[0, hd, LANES:, r] = ones

    _staged(IN_SPLIT, front, back)


def _even_in_proj(x, w_in_p, q_norm, kv_norm, w_uq_p, w_ukv_p, rope_c, rope_sn, rope_sp):
    B, S, D = x.shape
    tm = min(IN_SPLIT * TOKEN_TILE, S)
    nt = S // tm
    const = lambda b, i: (0, 0)
    head_out = lambda n, w=LANES: pl.BlockSpec((1, n, tm, w), lambda b, i: (b, 0, i, 0))
    head_shape = lambda n, w=LANES: jax.ShapeDtypeStruct((B, n, S, w), BF16)
    head_out_t = lambda n, r: pl.BlockSpec((1, n, r, tm), lambda b, i: (b, 0, 0, i))
    head_shape_t = lambda n, r: jax.ShapeDtypeStruct((B, n, r, S), BF16)
    return pl.pallas_call(
        _even_in_kernel,
        grid=(B, nt),
        in_specs=[
            pl.BlockSpec((1, tm, D), lambda b, i: (b, i, 0)),
            pl.BlockSpec(w_in_p.shape, const),
            pl.BlockSpec(q_norm.shape, const),
            pl.BlockSpec(kv_norm.shape, const),
            pl.BlockSpec(w_uq_p.shape, const),
            pl.BlockSpec(w_ukv_p.shape, const),
            pl.BlockSpec((tm, LANES), lambda b, i: (i, 0)),
            pl.BlockSpec((tm, LANES), lambda b, i: (i, 0)),
            pl.BlockSpec((tm, LANES), lambda b, i: (i, 0)),
        ],
        out_specs=[head_out(MLA_HEADS), head_out(MLA_HEADS), head_out_t(MLA_HEADS, LANES),
                   head_out(DIFF_HEADS), head_out(DIFF_HEADS), head_out_t(DIFF_HEADS, LANES + ONES_ROWS)],
        out_shape=[head_shape(MLA_HEADS), head_shape(MLA_HEADS), head_shape_t(MLA_HEADS, LANES),
                   head_shape(DIFF_HEADS), head_shape(DIFF_HEADS),
                   head_shape_t(DIFF_HEADS, LANES + ONES_ROWS)],
        compiler_params=_params("parallel", "parallel"),
        name="even_in_proj",
    )(x, w_in_p, q_norm, kv_norm, w_uq_p, w_ukv_p, rope_c, rope_sn, rope_sp)


def _mla_attn_kernel(q_ref, k_ref, vt_ref, o_ref, kmax_ref):
    S, tq = k_ref.shape[2], q_ref.shape[2]
    unit, kc = min(ATTN_UNIT, tq), min(ATTN_KEY_CHUNK, S)

    @pl.when(pl.program_id(2) == 0)
    def _():
        for hh in range(2):
            kmax_ref[hh] = _max_sq_norm_tile(k_ref[0, hh])

    units = [(hh, u) for u in range(tq // unit) for hh in range(2)]

    def q_of(i):
        hh, u = units[i]
        return q_ref[0, hh, u * unit:(u + 1) * unit, :]

    bounds = [jnp.sqrt(_row_sq_norms(q_of(i)) * kmax_ref[units[i][0]][0:1, 0:1]) * BOUND_SLACK
              for i in range(len(units))]

    def qk_chunk(i, c):
        return _dot_nt(k_ref[0, units[i][0], c * kc:(c + 1) * kc, :], q_of(i))

    def pv_chunk(i, c, scores):
        pt = jnp.exp2(scores[c] - bounds[i]).astype(BF16)
        return _dot(vt_ref[0, units[i][0], :, c * kc:(c + 1) * kc], pt)

    def write(accs):
        for u in range(tq // unit):
            ot = jnp.concatenate([a[HALF:] / a[:HALF] for a in accs[2 * u:2 * u + 2]], axis=0)
            o_ref[0, u * unit:(u + 1) * unit, :] = ot.T.astype(o_ref.dtype)

    accs = _pipelined_units(len(units), S // kc, qk_chunk, pv_chunk)
    write(accs)
    sums = accs[0][:8]
    for a in accs[1:]:
        sums = jnp.minimum(sums, a[:8])
    healthy = _all_rows_healthy(sums)

    @pl.when(jnp.logical_not(healthy))
    def _():
        exact = []
        for i, (hh, _) in enumerate(units):
            st = _dot_nt(k_ref[0, hh], q_of(i))
            pt = jnp.exp2(st - jnp.max(st, axis=0, keepdims=True)).astype(BF16)
            exact.append(_dot(vt_ref[0, hh], pt))
        write(exact)


def _mla_attention(q, k, v):
    B, H, S, _ = q.shape
    tq = min(FULL_QUERY_TILE, S)
    return pl.pallas_call(
        _mla_attn_kernel,
        grid=(B, H // 2, S // tq),
        in_specs=[
            pl.BlockSpec((1, 2, tq, LANES), lambda b, j, i: (b, j, i, 0)),
            pl.BlockSpec((1, 2, S, LANES), lambda b, j, i: (b, j, 0, 0)),
            pl.BlockSpec((1, 2, LANES, S), lambda b, j, i: (b, j, 0, 0)),
        ],
        out_specs=pl.BlockSpec((1, tq, LANES), lambda b, j, i: (b, i, j)),
        out_shape=jax.ShapeDtypeStruct((B, S, H * MLA_V), BF16),
        scratch_shapes=[pltpu.VMEM((2, 8, LANES), F32)],
        compiler_params=_params("parallel", "parallel", "arbitrary"),
        name="mla_attention",
    )(q, k, v)


def _diff_attn_kernel(slope_ref, lam_ref, q_ref, k_ref, vt_ref, g_ref, o_ref, kmax_ref, *, lambda_init):
    hd = pl.program_id(1)
    qi = pl.program_id(2)
    S, tq = k_ref.shape[2], q_ref.shape[2]
    unit, kc = min(ATTN_UNIT, tq), min(ATTN_KEY_CHUNK, S)
    slope = slope_ref[hd]

    @pl.when(qi == 0)
    def _():
        k = k_ref[0, 0]
        k_lane = lax.broadcasted_iota(jnp.int32, k.shape, 1)
        k_zero = jnp.zeros_like(k)
        kmax_ref[0] = _max_sq_norm_tile(jnp.where(k_lane < HALF, k, k_zero))
        kmax_ref[1] = _max_sq_norm_tile(jnp.where(k_lane >= HALF, k, k_zero))

    lv = lam_ref[...]
    lam = (jnp.exp(jnp.sum(lv[0:1] * lv[1:2], axis=-1, keepdims=True))
           - jnp.exp(jnp.sum(lv[2:3] * lv[3:4], axis=-1, keepdims=True)) + lambda_init)

    units = [(mp, u) for u in range(tq // unit) for mp in range(2)]
    lane = lax.broadcasted_iota(jnp.int32, (unit, LANES), 1)

    def q_of(i):
        mp, u = units[i]
        q = q_ref[0, 0, u * unit:(u + 1) * unit, :]
        return jnp.where((lane < HALF) if mp == 0 else (lane >= HALF), q, jnp.zeros_like(q))

    def q_pos(i):
        u = units[i][1]
        return (qi * tq + u * unit + lax.broadcasted_iota(jnp.int32, (1, unit), 1)).astype(F32) * slope

    def k_pos(c, n):
        return (c + lax.broadcasted_iota(jnp.int32, (n, 1), 0)).astype(F32) * slope

    bounds = [jnp.sqrt(_row_sq_norms(q_of(i)) * kmax_ref[units[i][0]][0:1, 0:1]) * BOUND_SLACK
              for i in range(len(units))]

    def qk_chunk(i, c):
        return _dot_nt(k_ref[0, 0, c * kc:(c + 1) * kc, :], q_of(i))

    def pv_chunk(i, c, scores):
        dist = jnp.abs(k_pos(c * kc, kc) - q_pos(i))
        pt = jnp.exp2(scores[c] - dist - bounds[i]).astype(BF16)
        return _dot(vt_ref[0, 0, :, c * kc:(c + 1) * kc], pt)

    def write(accs):
        for u in range(tq // unit):
            outs = [a[:LANES] / a[LANES:LANES + 1] for a in accs[2 * u:2 * u + 2]]
            o = (outs[0] - lam * outs[1]).T
            o = _rms(o, g_ref[...]) * (1.0 - lambda_init)
            o_ref[0, u * unit:(u + 1) * unit, :] = o.astype(o_ref.dtype)

    accs = _pipelined_units(len(units), S // kc, qk_chunk, pv_chunk)
    write(accs)
    sums = accs[0][LANES:LANES + 8]
    for a in accs[1:]:
        sums = jnp.minimum(sums, a[LANES:LANES + 8])
    healthy = _all_rows_healthy(sums)

    @pl.when(jnp.logical_not(healthy))
    def _():
        exact = []
        for i in range(len(units)):
            st = _dot_nt(k_ref[0, 0], q_of(i)) - jnp.abs(k_pos(0, S) - q_pos(i))
            pt = jnp.exp2(st - jnp.max(st, axis=0, keepdims=True)).astype(BF16)
            exact.append(_dot(vt_ref[0, 0], pt))
        write(exact)


def _diff_attention(slopes, lam_vecs, q, k, v, diff_norm, lambda_init):
    B, H, S, _ = q.shape
    tq = min(FULL_QUERY_TILE, S)
    smem = pl.BlockSpec(memory_space=pltpu.SMEM)
    return pl.pallas_call(
        functools.partial(_diff_attn_kernel, lambda_init=lambda_init),
        grid=(B, H, S // tq),
        in_specs=[
            smem,
            pl.BlockSpec(lam_vecs.shape, lambda b, h, i: (0, 0)),
            pl.BlockSpec((1, 1, tq, LANES), lambda b, h, i: (b, h, i, 0)),
            pl.BlockSpec((1, 1, S, LANES), lambda b, h, i: (b, h, 0, 0)),
            pl.BlockSpec((1, 1, LANES + ONES_ROWS, S), lambda b, h, i: (b, h, 0, 0)),
            pl.BlockSpec(diff_norm.shape, lambda b, h, i: (0, 0)),
        ],
        out_specs=pl.BlockSpec((1, tq, LANES), lambda b, h, i: (b, i, h)),
        out_shape=jax.ShapeDtypeStruct((B, S, H * DIFF_VD), BF16),
        scratch_shapes=[pltpu.VMEM((2, 8, LANES), F32)],
        compiler_params=_params("parallel", "parallel", "arbitrary"),
        name="diff_attention",
    )(slopes, lam_vecs, q, k, v, diff_norm)


def _odd_in_kernel(x_ref, w_ref, q_ref, k_ref, v_ref):
    hm = x_ref.shape[1] // IN_SPLIT
    nq = SWA_HEADS // 2
    nkv = SWA_KV_HEADS // 2
    q_scale = SWA_HD ** -0.5 * LOG2E
    low = lax.broadcasted_iota(jnp.int32, (hm, LANES), 1) < HALF
    ones = jnp.ones((ONES_ROWS, hm), BF16)

    def front(i):
        return _dot(x_ref[0, i * hm:(i + 1) * hm, :].astype(BF16), w_ref[...])

    def back(i, h):
        r = slice(i * hm, (i + 1) * hm)
        for pr in range(nq):
            q_ref[0, pr, r, :] = (h[:, pr * LANES:(pr + 1) * LANES] * q_scale).astype(BF16)
        for j in range(nkv):
            kp = h[:, (nq + j) * LANES:(nq + j + 1) * LANES]
            swapped = pltpu.roll(kp, HALF, 1)
            k_ref[0, 2 * j, r, :] = jnp.where(low, kp, swapped).astype(BF16)
            k_ref[0, 2 * j + 1, r, :] = jnp.where(low, swapped, kp).astype(BF16)
            vt = h[:, (nq + nkv + j) * LANES:(nq + nkv + j + 1) * LANES].T.astype(BF16)
            for half in range(2):
                v_ref[0, 2 * j + half, :SWA_HD, r] = vt[half * SWA_HD:(half + 1) * SWA_HD]
                v_ref[0, 2 * j + half, SWA_HD:, r] = ones

    _staged(IN_SPLIT, front, back)


def _odd_in_proj(x, w_p):
    B, S, D = x.shape
    tm = min(IN_SPLIT * TOKEN_TILE, S)
    head_out = lambda n, w=LANES: pl.BlockSpec((1, n, tm, w), lambda b, i: (b, 0, i, 0))
    head_shape = lambda n, w=LANES: jax.ShapeDtypeStruct((B, n, S, w), BF16)
    return pl.pallas_call(
        _odd_in_kernel,
        grid=(B, S // tm),
        in_specs=[
            pl.BlockSpec((1, tm, D), lambda b, i: (b, i, 0)),
            pl.BlockSpec(w_p.shape, lambda b, i: (0, 0)),
        ],
        out_specs=[head_out(SWA_HEADS // 2), head_out(SWA_KV_HEADS),
                   pl.BlockSpec((1, SWA_KV_HEADS, SWA_HD + ONES_ROWS, tm), lambda b, i: (b, 0, 0, i))],
        out_shape=[head_shape(SWA_HEADS // 2), head_shape(SWA_KV_HEADS),
                   jax.ShapeDtypeStruct((B, SWA_KV_HEADS, SWA_HD + ONES_ROWS, S), BF16)],
        compiler_params=_params("parallel", "parallel"),
        name="odd_in_proj",
    )(x, w_p)


SWA_UNIT_BLOCKS = 2
SWA_UNIT_SPAN = SWA_UNIT_BLOCKS * BLOCK + 2 * WINDOW
SWA_KEY_CHUNK = 512
_SWA_SHIFTS = (-WINDOW, 0, -2 * WINDOW)


def _swa_attn_kernel(slope_ref, sink_ref, q_ref, k_ref, vt_ref, o_ref, bias_ref):
    g = pl.program_id(0)
    S = k_ref.shape[2]
    span = SWA_UNIT_SPAN
    n_units = S // (SWA_UNIT_BLOCKS * BLOCK)
    kc = SWA_KEY_CHUNK
    n_chunks = span // kc
    cols = SWA_UNIT_BLOCKS * SWA_GROUP * BLOCK
    low = lax.broadcasted_iota(jnp.int32, (BLOCK, LANES), 1) < HALF
    col = lax.broadcasted_iota(jnp.int32, (1, cols), 1)
    head_of_col = (col // BLOCK) % SWA_GROUP
    slope_row = jnp.zeros((1, cols), F32)
    sink_row = jnp.zeros((1, cols), F32)
    for hh in range(SWA_GROUP):
        slope_row = jnp.where(head_of_col == hh, slope_ref[g * SWA_GROUP + hh], slope_row)
        sink_row = jnp.where(head_of_col == hh, sink_ref[g * SWA_GROUP + hh] * LOG2E, sink_row)

    @pl.when(pl.program_id(1) == 0)
    def _():
        q_in_unit = col // (SWA_GROUP * BLOCK) * BLOCK + col % BLOCK
        rel0 = lax.broadcasted_iota(jnp.int32, (span, cols), 0) - q_in_unit
        for t, shift in enumerate(_SWA_SHIFTS):
            dist = jnp.abs(rel0 + shift)
            bias_ref[t] = jnp.where(dist <= WINDOW, -slope_row * dist.astype(F32), -jnp.inf)

    def first_query(u):
        return u * SWA_UNIT_BLOCKS * BLOCK

    def window_start(u):
        return min(max(first_query(u) - WINDOW, 0), S - span)

    def stacked_q(u):
        parts = []
        for blk in range(SWA_UNIT_BLOCKS):
            lo = first_query(u) + blk * BLOCK
            for pr in range(2):
                qp = q_ref[0, pr, lo:lo + BLOCK, :]
                zero = jnp.zeros_like(qp)
                parts += [jnp.where(low, qp, zero), jnp.where(low, zero, qp)]
        return jnp.concatenate(parts, axis=0)

    col_max = {}

    def qk_chunk(u, c):
        lo = window_start(u) + c * kc
        t = _SWA_SHIFTS.index(window_start(u) - first_query(u))
        st = _dot_nt(k_ref[0, 0, lo:lo + kc, :], stacked_q(u)) + bias_ref[t, c * kc:(c + 1) * kc, :]
        col_max[u] = jnp.maximum(col_max.get(u, sink_row), jnp.max(st, axis=0, keepdims=True))
        return st

    def pv_chunk(u, c, scores):
        lo = window_start(u) + c * kc
        return _dot(vt_ref[0, 0, :, lo:lo + kc], jnp.exp2(scores[c] - col_max[u]).astype(BF16))

    def finish(u, acc):
        m = col_max.pop(u)
        ot = acc[:SWA_HD] / (acc[SWA_HD:SWA_HD + 1] + jnp.exp2(sink_row - m))
        for blk in range(SWA_UNIT_BLOCKS):
            lo = first_query(u) + blk * BLOCK
            for pr in range(2):
                c0 = (blk * SWA_GROUP + 2 * pr) * BLOCK
                pair = jnp.concatenate([ot[:, c0:c0 + BLOCK], ot[:, c0 + BLOCK:c0 + 2 * BLOCK]], axis=0)
                o_ref[0, lo:lo + BLOCK, pr * LANES:(pr + 1) * LANES] = pair.T.astype(o_ref.dtype)

    _pipelined_units(n_units, n_chunks, qk_chunk, pv_chunk, finish)


def _swa_attention(slopes, sink, q, k, v):
    B, _, S, _ = q.shape
    smem = pl.BlockSpec(memory_space=pltpu.SMEM)
    return pl.pallas_call(
        _swa_attn_kernel,
        grid=(SWA_KV_HEADS, B),
        in_specs=[
            smem, smem,
            pl.BlockSpec((1, 2, S, LANES), lambda g, b: (b, g, 0, 0)),
            pl.BlockSpec((1, 1, S, LANES), lambda g, b: (b, g, 0, 0)),
            pl.BlockSpec((1, 1, SWA_HD + ONES_ROWS, S), lambda g, b: (b, g, 0, 0)),
        ],
        out_specs=pl.BlockSpec((1, S, 2 * LANES), lambda g, b: (b, 0, g)),
        out_shape=jax.ShapeDtypeStruct((B, S, SWA_HEADS * SWA_HD), BF16),
        scratch_shapes=[pltpu.VMEM((len(_SWA_SHIFTS), SWA_UNIT_SPAN,
                                    SWA_UNIT_BLOCKS * SWA_GROUP * BLOCK), F32)],
        compiler_params=_params("arbitrary", "arbitrary"),
        name="swa_attention",
    )(slopes, sink, q, k, v)


def _post_kernel(oa_ref, ob_ref, x_ref, woa_ref, wob_ref, g1_ref, b1_ref,
                 w1_ref, fb1_ref, w2_ref, fb2_ref, g2_ref, b2_ref, out_ref):
    hm = x_ref.shape[0] // POST_SPLIT
    rows = [slice(h * hm, (h + 1) * hm) for h in range(POST_SPLIT)]
    y = [_dot(oa_ref[r, :], woa_ref[...]) + _dot(ob_ref[r, :], wob_ref[...]) for r in rows]
    x1 = [_layer_norm(ALPHA * x_ref[r, :] + y[h], g1_ref[...], b1_ref[...]) for h, r in enumerate(rows)]
    x1b = [v.astype(BF16) for v in x1]
    acc = [jnp.zeros_like(v) + fb2_ref[...] for v in x1]
    for c in range(0, w1_ref.shape[1], FF_CHUNK):
        for h in range(POST_SPLIT):
            hc = _dot(x1b[h], w1_ref[:, c:c + FF_CHUNK]) + fb1_ref[:, c:c + FF_CHUNK]
            hc = jnp.square(jnp.maximum(hc, 0.0))
            acc[h] = acc[h] + _dot(hc.astype(BF16), w2_ref[c:c + FF_CHUNK, :])
    for h, r in enumerate(rows):
        out_ref[r, :] = _layer_norm(ALPHA * x1[h] + acc[h], g2_ref[...], b2_ref[...])


def _post(o_a, o_b, a_blk, b_blk, x, wo_a, wo_b, g1, b1, w1, fb1, w2, fb2, g2, b2):
    T, D = x.shape
    tm = min(POST_SPLIT * POST_TILE, T)
    half = wo_a.shape[0]
    const = lambda i: (0, 0)
    resident = lambda a: pl.BlockSpec(a.shape, const, pipeline_mode=pl.Buffered(1))
    return pl.pallas_call(
        _post_kernel,
        grid=(T // tm,),
        in_specs=[
            pl.BlockSpec((tm, half), lambda i: (i, a_blk)),
            pl.BlockSpec((tm, half), lambda i: (i, b_blk)),
            pl.BlockSpec((tm, D), lambda i: (i, 0)),
            resident(wo_a), resident(wo_b), resident(g1), resident(b1),
            resident(w1), resident(fb1), resident(w2), resident(fb2), resident(g2), resident(b2),
        ],
        out_specs=pl.BlockSpec((tm, D), lambda i: (i, 0)),
        out_shape=jax.ShapeDtypeStruct((T, D), F32),
        compiler_params=_params("parallel"),
        name="post_mlp",
    )(o_a, o_b, x, wo_a, wo_b, g1, b1, w1, fb1, w2, fb2, g2, b2)


def _prep_even_weights(w_in, w_uq, w_ukv):
    D = w_in.shape[0]
    o_kr = MLA_Q_LORA + MLA_KV_LORA
    o_dq = o_kr + MLA_ROPE
    kr_slab = jnp.zeros((D, LANES), F32).at[:, MLA_NOPE:MLA_NOPE + MLA_ROPE].set(w_in[:, o_kr:o_dq])
    w_in_p = jnp.concatenate([w_in[:, :o_kr], kr_slab, w_in[:, o_dq:]], axis=1).astype(BF16)
    assert w_in_p.shape[1] == _EV_END

    r = w_uq.shape[0]
    qh = w_uq.reshape(r, MLA_HEADS, MLA_NOPE + MLA_ROPE)
    qh = jnp.pad(qh, ((0, 0), (0, 0), (0, LANES - MLA_NOPE - MLA_ROPE)))
    w_uq_p = qh.reshape(r, MLA_HEADS * LANES).astype(BF16)

    assert MLA_NOPE + MLA_V == LANES
    return w_in_p, w_uq_p, w_ukv.astype(BF16)


def _rope_tables(S):
    half = MLA_ROPE // 2
    inv = ROPE_THETA ** (-np.arange(half, dtype=np.float32) / half)
    ang = np.arange(S, dtype=np.float32)[:, None] * inv[None, :]
    cos, sin = np.cos(ang).astype(np.float32), np.sin(ang).astype(np.float32)
    c = np.zeros((S, LANES), np.float32)
    sn = np.zeros((S, LANES), np.float32)
    sp = np.zeros((S, LANES), np.float32)
    c[:, :MLA_NOPE] = 1.0
    c[:, MLA_NOPE:MLA_NOPE + half] = cos
    c[:, MLA_NOPE + half:MLA_NOPE + 2 * half] = cos
    sn[:, MLA_NOPE:MLA_NOPE + half] = -sin
    sp[:, MLA_NOPE + half:MLA_NOPE + 2 * half] = sin
    return jnp.asarray(c), jnp.asarray(sn), jnp.asarray(sp)


def _alibi_slopes_log2(n):
    return jnp.asarray(2.0 ** (-8.0 * np.arange(1, n + 1, dtype=np.float32) / n) * LOG2E, F32)


def kernel(x, ev_w_in, ev_q_norm, ev_kv_norm, ev_w_uq, ev_w_ukv, ev_lam_q1, ev_lam_k1, ev_lam_q2, ev_lam_k2, ev_diff_norm, ev_w_out, od_w_in, od_sink, od_w_out, ln1_g, ln1_b, ln2_g, ln2_b, ffn_w1, ffn_b1, ffn_w2, ffn_b2):
    B, S, D = x.shape
    row = lambda a: a.reshape(1, -1).astype(F32)
    rope_c, rope_sn, rope_sp = _rope_tables(S)
    xs = x
    for layer in range(DEPTH):
        j = layer // 2
        if layer % 2 == 0:
            lambda_init = 0.8 - 0.6 * math.exp(-0.3 * layer)
            w_in_p, w_uq_p, w_ukv_p = _prep_even_weights(ev_w_in[j], ev_w_uq[j], ev_w_ukv[j])
            q, k, v, dq, dk, dv = _even_in_proj(xs, w_in_p, row(ev_q_norm[j]), row(ev_kv_norm[j]),
                                                w_uq_p, w_ukv_p, rope_c, rope_sn, rope_sp)
            o_mla = _mla_attention(q, k, v)
            lam_vecs = jnp.stack([ev_lam_q1[j], ev_lam_k1[j], ev_lam_q2[j], ev_lam_k2[j]]).astype(F32)
            o_diff = _diff_attention(_alibi_slopes_log2(DIFF_HEADS), lam_vecs, dq, dk, dv,
                                     row(ev_diff_norm[j]), lambda_init)
            n_a = o_mla.shape[-1]
            o_a, o_b, a_blk, b_blk = o_mla.reshape(B * S, n_a), o_diff.reshape(B * S, -1), 0, 0
            w_out = ev_w_out[j]
        else:
            q, k, v = _odd_in_proj(xs, od_w_in[j].astype(BF16))
            o = _swa_attention(_alibi_slopes_log2(SWA_HEADS), od_sink[j].astype(F32), q, k, v)
            n_a = o.shape[-1] // 2
            o_a = o_b = o.reshape(B * S, -1)
            a_blk, b_blk = 0, 1
            w_out = od_w_out[j]
        xs = _post(o_a, o_b, a_blk, b_blk, xs.reshape(B * S, D),
                   w_out[:n_a].astype(BF16), w_out[n_a:].astype(BF16),
                   row(ln1_g[layer]), row(ln1_b[layer]),
                   ffn_w1[layer].astype(BF16), row(ffn_b1[layer]),
                   ffn_w2[layer].astype(BF16), row(ffn_b2[layer]),
                   row(ln2_g[layer]), row(ln2_b[layer])).reshape(B, S, D)
    return xs
```

```python
import functools
import math

import jax
import jax.numpy as jnp
import numpy as np
from jax import lax
from jax.experimental import pallas as pl
from jax.experimental.pallas import tpu as pltpu

F32 = jnp.float32
BF16 = jnp.bfloat16

D_MODEL = 1024
DEPTH = 2
MLA_HEADS = 8
MLA_NOPE = 64
MLA_ROPE = 32
MLA_V = 64
MLA_Q_LORA = 384
MLA_KV_LORA = 256
ROPE_THETA = 10000.0
DIFF_HEADS = 4
DIFF_HD = 64
DIFF_VD = 2 * DIFF_HD
SWA_HEADS = 16
SWA_KV_HEADS = 4
SWA_HD = 64
SWA_GROUP = SWA_HEADS // SWA_KV_HEADS
WINDOW = 128
BLOCK = 128
D_FF = 4 * D_MODEL
ALPHA = (2 * DEPTH) ** 0.25
LN_EPS = 1e-5
RMS_EPS = 1e-6
LOG2E = math.log2(math.e)

LANES = 128
HALF = LANES // 2

_EV_CQ = 0
_EV_CKV = _EV_CQ + MLA_Q_LORA
_EV_KR = _EV_CKV + MLA_KV_LORA
_EV_DQ = _EV_KR + LANES
_EV_DK = _EV_DQ + DIFF_HEADS * LANES
_EV_DV = _EV_DK + DIFF_HEADS * LANES
_EV_END = _EV_DV + DIFF_HEADS * LANES

VMEM_LIMIT_BYTES = 56 * 1024 * 1024

TOKEN_TILE = 512
POST_TILE = 256
FULL_QUERY_TILE = 2048
ATTN_UNIT = 512
ATTN_KEY_CHUNK = 256
ONES_ROWS = 16
FF_CHUNK = 1024
POST_SPLIT = 4
IN_SPLIT = 2

BOUND_SLACK = 1.0 + 2.0 ** -6
MIN_ROW_SUM = 2.0 ** -100


def _params(*sem):
    return pltpu.CompilerParams(dimension_semantics=sem, vmem_limit_bytes=VMEM_LIMIT_BYTES)


def _dot(a, b):
    return jnp.dot(a, b, preferred_element_type=F32)


def _dot_nt(a, b):
    return lax.dot_general(a, b, (((1,), (1,)), ((), ())), preferred_element_type=F32)


def _rms(x, g):
    return x * lax.rsqrt(jnp.mean(x * x, axis=-1, keepdims=True) + RMS_EPS) * g


def _layer_norm(x, g, b):
    mu = jnp.mean(x, axis=-1, keepdims=True)
    xc = x - mu
    var = jnp.mean(xc * xc, axis=-1, keepdims=True)
    return xc * lax.rsqrt(var + LN_EPS) * g + b


def _rope_slab(blk, c, s_next, s_prev):
    nxt = pltpu.roll(blk, LANES - MLA_ROPE // 2, 1)
    prv = pltpu.roll(blk, MLA_ROPE // 2, 1)
    return blk * c + nxt * s_next + prv * s_prev


def _sq_norm(x):
    xf = x.astype(F32)
    return jnp.sum(xf * xf, axis=1, keepdims=True)


def _max_sq_norm_tile(x):
    return jnp.broadcast_to(jnp.max(_sq_norm(x), axis=0, keepdims=True), (8, LANES))


def _row_sq_norms(x):
    ones = jnp.ones((8, x.shape[1]), BF16)
    return _dot_nt(ones, x * x)[0:1]


def _pipelined_units(n_units, n_chunks, qk_chunk, pv_chunk, finish=None):
    results = []
    cur = None
    for stage in range(n_units + 1):
        nxt = [] if stage < n_units else None
        acc = None
        for c in range(n_chunks):
            if nxt is not None:
                nxt.append(qk_chunk(stage, c))
            if cur is not None:
                d = pv_chunk(stage - 1, c, cur)
                acc = d if acc is None else acc + d
        if cur is not None:
            if finish is None:
                results.append(acc)
            else:
                finish(stage - 1, acc)
        cur = nxt
    return results


def _staged(n, front, back):
    carried = None
    for i in range(n + 1):
        nxt = front(i) if i < n else None
        if carried is not None:
            back(i - 1, carried)
        carried = nxt


def _all_rows_healthy(row_sums):
    return jnp.min(jnp.where(row_sums >= MIN_ROW_SUM, 1.0, 0.0)) > 0.5


def _even_in_kernel(x_ref, w_in_ref, qn_ref, kvn_ref, w_uq_ref, w_ukv_ref, c_ref, sn_ref, sp_ref,
                    q_ref, k_ref, v_ref, dq_ref, dk_ref, dv_ref):
    hm = x_ref.shape[1] // IN_SPLIT
    low = lax.broadcasted_iota(jnp.int32, (hm, LANES), 1) < HALF
    ones = jnp.ones((ONES_ROWS, hm), BF16)
    q_scale = (MLA_NOPE + MLA_ROPE) ** -0.5 * LOG2E
    dq_scale = DIFF_HD ** -0.5 * LOG2E

    def front(i):
        h = _dot(x_ref[0, i * hm:(i + 1) * hm, :].astype(BF16), w_in_ref[...])
        cq = _rms(h[:, _EV_CQ:_EV_CKV], qn_ref[...])
        q = _dot(cq.astype(BF16), w_uq_ref[...])
        ckv = _rms(h[:, _EV_CKV:_EV_KR], kvn_ref[...])
        kv = _dot(ckv.astype(BF16), w_ukv_ref[...])
        return h, q, kv

    def back(i, vals):
        h, q, kv = vals
        r = slice(i * hm, (i + 1) * hm)
        c, sn, sp = c_ref[r, :], sn_ref[r, :], sp_ref[r, :]
        for hd in range(MLA_HEADS):
            blk = q[:, hd * LANES:(hd + 1) * LANES]
            q_ref[0, hd, r, :] = (_rope_slab(blk, c, sn, sp) * q_scale).astype(BF16)
        kr = _rope_slab(h[:, _EV_KR:_EV_DQ], c, sn, sp)
        for hd in range(MLA_HEADS):
            slab = kv[:, hd * LANES:(hd + 1) * LANES]
            k_ref[0, hd, r, :] = (jnp.where(low, slab, 0.0) + kr).astype(BF16)
            v_ref[0, hd, :, r] = jnp.where(low, 1.0, slab).T.astype(BF16)
        for hd in range(DIFF_HEADS):
            dq_ref[0, hd, r, :] = (h[:, _EV_DQ + hd * LANES:_EV_DQ + (hd + 1) * LANES] * dq_scale).astype(BF16)
            dk_ref[0, hd, r, :] = h[:, _EV_DK + hd * LANES:_EV_DK + (hd + 1) * LANES].astype(BF16)
            dv_ref[0, hd, :LANES, r] = h[:, _EV_DV + hd * LANES:_EV_DV + (hd + 1) * LANES].T.astype(BF16)
            dv_ref[0, hd, LANES:, r] = ones

    _staged(IN_SPLIT, front, back)


def _even_in_proj(x, w_in_p, q_norm, kv_norm, w_uq_p, w_ukv_p, rope_c, rope_sn, rope_sp):
    B, S, D = x.shape
    tm = min(IN_SPLIT * TOKEN_TILE, S)
    nt = S // tm
    const = lambda b, i: (0, 0)
    head_out = lambda n, w=LANES: pl.BlockSpec((1, n, tm, w), lambda b, i: (b, 0, i, 0))
    head_shape = lambda n, w=LANES: jax.ShapeDtypeStruct((B, n, S, w), BF16)
    head_out_t = lambda n, r: pl.BlockSpec((1, n, r, tm), lambda b, i: (b, 0, 0, i))
    head_shape_t = lambda n, r: jax.ShapeDtypeStruct((B, n, r, S), BF16)
    return pl.pallas_call(
        _even_in_kernel,
        grid=(B, nt),
        in_specs=[
            pl.BlockSpec((1, tm, D), lambda b, i: (b, i, 0)),
            pl.BlockSpec(w_in_p.shape, const),
            pl.BlockSpec(q_norm.shape, const),
            pl.BlockSpec(kv_norm.shape, const),
            pl.BlockSpec(w_uq_p.shape, const),
            pl.BlockSpec(w_ukv_p.shape, const),
            pl.BlockSpec((tm, LANES), lambda b, i: (i, 0)),
            pl.BlockSpec((tm, LANES), lambda b, i: (i, 0)),
            pl.BlockSpec((tm, LANES), lambda b, i: (i, 0)),
        ],
        out_specs=[head_out(MLA_HEADS), head_out(MLA_HEADS), head_out_t(MLA_HEADS, LANES),
                   head_out(DIFF_HEADS), head_out(DIFF_HEADS), head_out_t(DIFF_HEADS, LANES + ONES_ROWS)],
        out_shape=[head_shape(MLA_HEADS), head_shape(MLA_HEADS), head_shape_t(MLA_HEADS, LANES),
                   head_shape(DIFF_HEADS), head_shape(DIFF_HEADS),
                   head_shape_t(DIFF_HEADS, LANES + ONES_ROWS)],
        compiler_params=_params("parallel", "parallel"),
        name="even_in_proj",
    )(x, w_in_p, q_norm, kv_norm, w_uq_p, w_ukv_p, rope_c, rope_sn, rope_sp)


def _mla_attn_kernel(q_ref, k_ref, vt_ref, o_ref, kmax_ref):
    S, tq = k_ref.shape[2], q_ref.shape[2]
    unit, kc = min(ATTN_UNIT, tq), min(ATTN_KEY_CHUNK, S)

    @pl.when(pl.program_id(2) == 0)
    def _():
        for hh in range(2):
            kmax_ref[hh] = _max_sq_norm_tile(k_ref[0, hh])

    units = [(hh, u) for u in range(tq // unit) for hh in range(2)]

    def q_of(i):
        hh, u = units[i]
        return q_ref[0, hh, u * unit:(u + 1) * unit, :]

    bounds = [jnp.sqrt(_row_sq_norms(q_of(i)) * kmax_ref[units[i][0]][0:1, 0:1]) * BOUND_SLACK
              for i in range(len(units))]

    def qk_chunk(i, c):
        return _dot_nt(k_ref[0, units[i][0], c * kc:(c + 1) * kc, :], q_of(i))

    def pv_chunk(i, c, scores):
        pt = jnp.exp2(scores[c] - bounds[i]).astype(BF16)
        return _dot(vt_ref[0, units[i][0], :, c * kc:(c + 1) * kc], pt)

    def write(accs):
        for u in range(tq // unit):
            ot = jnp.concatenate([a[HALF:] / a[:HALF] for a in accs[2 * u:2 * u + 2]], axis=0)
            o_ref[0, u * unit:(u + 1) * unit, :] = ot.T.astype(o_ref.dtype)

    accs = _pipelined_units(len(units), S // kc, qk_chunk, pv_chunk)
    write(accs)
    sums = accs[0][:8]
    for a in accs[1:]:
        sums = jnp.minimum(sums, a[:8])
    healthy = _all_rows_healthy(sums)

    @pl.when(jnp.logical_not(healthy))
    def _():
        exact = []
        for i, (hh, _) in enumerate(units):
            st = _dot_nt(k_ref[0, hh], q_of(i))
            pt = jnp.exp2(st - jnp.max(st, axis=0, keepdims=True)).astype(BF16)
            exact.append(_dot(vt_ref[0, hh], pt))
        write(exact)


def _mla_attention(q, k, v):
    B, H, S, _ = q.shape
    tq = min(FULL_QUERY_TILE, S)
    return pl.pallas_call(
        _mla_attn_kernel,
        grid=(B, H // 2, S // tq),
        in_specs=[
            pl.BlockSpec((1, 2, tq, LANES), lambda b, j, i: (b, j, i, 0)),
            pl.BlockSpec((1, 2, S, LANES), lambda b, j, i: (b, j, 0, 0)),
            pl.BlockSpec((1, 2, LANES, S), lambda b, j, i: (b, j, 0, 0)),
        ],
        out_specs=pl.BlockSpec((1, tq, LANES), lambda b, j, i: (b, i, j)),
        out_shape=jax.ShapeDtypeStruct((B, S, H * MLA_V), BF16),
        scratch_shapes=[pltpu.VMEM((2, 8, LANES), F32)],
        compiler_params=_params("parallel", "parallel", "arbitrary"),
        name="mla_attention",
    )(q, k, v)


def _diff_attn_kernel(slope_ref, lam_ref, q_ref, k_ref, vt_ref, g_ref, o_ref, kmax_ref, *, lambda_init):
    hd = pl.program_id(1)
    qi = pl.program_id(2)
    S, tq = k_ref.shape[2], q_ref.shape[2]
    unit, kc = min(ATTN_UNIT, tq), min(ATTN_KEY_CHUNK, S)
    slope = slope_ref[hd]

    @pl.when(qi == 0)
    def _():
        k = k_ref[0, 0]
        k_lane = lax.broadcasted_iota(jnp.int32, k.shape, 1)
        k_zero = jnp.zeros_like(k)
        kmax_ref[0] = _max_sq_norm_tile(jnp.where(k_lane < HALF, k, k_zero))
        kmax_ref[1] = _max_sq_norm_tile(jnp.where(k_lane >= HALF, k, k_zero))

    lv = lam_ref[...]
    lam = (jnp.exp(jnp.sum(lv[0:1] * lv[1:2], axis=-1, keepdims=True))
           - jnp.exp(jnp.sum(lv[2:3] * lv[3:4], axis=-1, keepdims=True)) + lambda_init)

    units = [(mp, u) for u in range(tq // unit) for mp in range(2)]
    lane = lax.broadcasted_iota(jnp.int32, (unit, LANES), 1)

    def q_of(i):
        mp, u = units[i]
        q = q_ref[0, 0, u * unit:(u + 1) * unit, :]
        return jnp.where((lane < HALF) if mp == 0 else (lane >= HALF), q, jnp.zeros_like(q))

    def q_pos(i):
        u = units[i][1]
        return (qi * tq + u * unit + lax.broadcasted_iota(jnp.int32, (1, unit), 1)).astype(F32) * slope

    def k_pos(c, n):
        return (c + lax.broadcasted_iota(jnp.int32, (n, 1), 0)).astype(F32) * slope

    bounds = [jnp.sqrt(_row_sq_norms(q_of(i)) * kmax_ref[units[i][0]][0:1, 0:1]) * BOUND_SLACK
              for i in range(len(units))]

    def qk_chunk(i, c):
        return _dot_nt(k_ref[0, 0, c * kc:(c + 1) * kc, :], q_of(i))

    def pv_chunk(i, c, scores):
        dist = jnp.abs(k_pos(c * kc, kc) - q_pos(i))
        pt = jnp.exp2(scores[c] - dist - bounds[i]).astype(BF16)
        return _dot(vt_ref[0, 0, :, c * kc:(c + 1) * kc], pt)

    def write(accs):
        for u in range(tq // unit):
            outs = [a[:LANES] / a[LANES:LANES + 1] for a in accs[2 * u:2 * u + 2]]
            o = (outs[0] - lam * outs[1]).T
            o = _rms(o, g_ref[...]) * (1.0 - lambda_init)
            o_ref[0, u * unit:(u + 1) * unit, :] = o.astype(o_ref.dtype)

    accs = _pipelined_units(len(units), S // kc, qk_chunk, pv_chunk)
    write(accs)
    sums = accs[0][LANES:LANES + 8]
    for a in accs[1:]:
        sums = jnp.minimum(sums, a[LANES:LANES + 8])
    healthy = _all_rows_healthy(sums)

    @pl.when(jnp.logical_not(healthy))
    def _():
        exact = []
        for i in range(len(units)):
            st = _dot_nt(k_ref[0, 0], q_of(i)) - jnp.abs(k_pos(0, S) - q_pos(i))
            pt = jnp.exp2(st - jnp.max(st, axis=0, keepdims=True)).astype(BF16)
            exact.append(_dot(vt_ref[0, 0], pt))
        write(exact)


def _diff_attention(slopes, lam_vecs, q, k, v, diff_norm, lambda_init):
    B, H, S, _ = q.shape
    tq = min(FULL_QUERY_TILE, S)
    smem = pl.BlockSpec(memory_space=pltpu.SMEM)
    return pl.pallas_call(
        functools.partial(_diff_attn_kernel, lambda_init=lambda_init),
        grid=(B, H, S // tq),
        in_specs=[
            smem,
            pl.BlockSpec(lam_vecs.shape, lambda b, h, i: (0, 0)),
            pl.BlockSpec((1, 1, tq, LANES), lambda b, h, i: (b, h, i, 0)),
            pl.BlockSpec((1, 1, S, LANES), lambda b, h, i: (b, h, 0, 0)),
            pl.BlockSpec((1, 1, LANES + ONES_ROWS, S), lambda b, h, i: (b, h, 0, 0)),
            pl.BlockSpec(diff_norm.shape, lambda b, h, i: (0, 0)),
        ],
        out_specs=pl.BlockSpec((1, tq, LANES), lambda b, h, i: (b, i, h)),
        out_shape=jax.ShapeDtypeStruct((B, S, H * DIFF_VD), BF16),
        scratch_shapes=[pltpu.VMEM((2, 8, LANES), F32)],
        compiler_params=_params("parallel", "parallel", "arbitrary"),
        name="diff_attention",
    )(slopes, lam_vecs, q, k, v, diff_norm)


def _odd_in_kernel(x_ref, w_ref, q_ref, k_ref, v_ref):
    hm = x_ref.shape[1] // IN_SPLIT
    nq = SWA_HEADS // 2
    nkv = SWA_KV_HEADS // 2
    q_scale = SWA_HD ** -0.5 * LOG2E
    low = lax.broadcasted_iota(jnp.int32, (hm, LANES), 1) < HALF
    ones = jnp.ones((ONES_ROWS, hm), BF16)

    def front(i):
        return _dot(x_ref[0, i * hm:(i + 1) * hm, :].astype(BF16), w_ref[...])

    def back(i, h):
        r = slice(i * hm, (i + 1) * hm)
        for pr in range(nq):
            q_ref[0, pr, r, :] = (h[:, pr * LANES:(pr + 1) * LANES] * q_scale).astype(BF16)
        for j in range(nkv):
            kp = h[:, (nq + j) * LANES:(nq + j + 1) * LANES]
            swapped = pltpu.roll(kp, HALF, 1)
            k_ref[0, 2 * j, r, :] = jnp.where(low, kp, swapped).astype(BF16)
            k_ref[0, 2 * j + 1, r, :] = jnp.where(low, swapped, kp).astype(BF16)
            vt = h[:, (nq + nkv + j) * LANES:(nq + nkv + j + 1) * LANES].T.astype(BF16)
            for half in range(2):
                v_ref[0, 2 * j + half, :SWA_HD, r] = vt[half * SWA_HD:(half + 1) * SWA_HD]
                v_ref[0, 2 * j + half, SWA_HD:, r] = ones

    _staged(IN_SPLIT, front, back)


def _odd_in_proj(x, w_p):
    B, S, D = x.shape
    tm = min(IN_SPLIT * TOKEN_TILE, S)
    head_out = lambda n, w=LANES: pl.BlockSpec((1, n, tm, w), lambda b, i: (b, 0, i, 0))
    head_shape = lambda n, w=LANES: jax.ShapeDtypeStruct((B, n, S, w), BF16)
    return pl.pallas_call(
        _odd_in_kernel,
        grid=(B, S // tm),
        in_specs=[
            pl.BlockSpec((1, tm, D), lambda b, i: (b, i, 0)),
            pl.BlockSpec(w_p.shape, lambda b, i: (0, 0)),
        ],
        out_specs=[head_out(SWA_HEADS // 2), head_out(SWA_KV_HEADS),
                   pl.BlockSpec((1, SWA_KV_HEADS, SWA_HD + ONES_ROWS, tm), lambda b, i: (b, 0, 0, i))],
        out_shape=[head_shape(SWA_HEADS // 2), head_shape(SWA_KV_HEADS),
                   jax.ShapeDtypeStruct((B, SWA_KV_HEADS, SWA_HD + ONES_ROWS, S), BF16)],
        compiler_params=_params("parallel", "parallel"),
        name="odd_in_proj",
    )(x, w_p)


SWA_UNIT_BLOCKS = 2
SWA_UNIT_SPAN = SWA_UNIT_BLOCKS * BLOCK + 2 * WINDOW
SWA_KEY_CHUNK = 512
_SWA_SHIFTS = (-WINDOW, 0, -2 * WINDOW)


def _swa_attn_kernel(slope_ref, sink_ref, q_ref, k_ref, vt_ref, o_ref, bias_ref):
    g = pl.program_id(0)
    S = k_ref.shape[2]
    span = SWA_UNIT_SPAN
    n_units = S // (SWA_UNIT_BLOCKS * BLOCK)
    kc = SWA_KEY_CHUNK
    n_chunks = span // kc
    cols = SWA_UNIT_BLOCKS * SWA_GROUP * BLOCK
    low = lax.broadcasted_iota(jnp.int32, (BLOCK, LANES), 1) < HALF
    col = lax.broadcasted_iota(jnp.int32, (1, cols), 1)
    head_of_col = (col // BLOCK) % SWA_GROUP
    slope_row = jnp.zeros((1, cols), F32)
    sink_row = jnp.zeros((1, cols), F32)
    for hh in range(SWA_GROUP):
        slope_row = jnp.where(head_of_col == hh, slope_ref[g * SWA_GROUP + hh], slope_row)
        sink_row = jnp.where(head_of_col == hh, sink_ref[g * SWA_GROUP + hh] * LOG2E, sink_row)

    @pl.when(pl.program_id(1) == 0)
    def _():
        q_in_unit = col // (SWA_GROUP * BLOCK) * BLOCK + col % BLOCK
        rel0 = lax.broadcasted_iota(jnp.int32, (span, cols), 0) - q_in_unit
        for t, shift in enumerate(_SWA_SHIFTS):
            dist = jnp.abs(rel0 + shift)
            bias_ref[t] = jnp.where(dist <= WINDOW, -slope_row * dist.astype(F32), -jnp.inf)

    def first_query(u):
        return u * SWA_UNIT_BLOCKS * BLOCK

    def window_start(u):
        return min(max(first_query(u) - WINDOW, 0), S - span)

    def stacked_q(u):
        parts = []
        for blk in range(SWA_UNIT_BLOCKS):
            lo = first_query(u) + blk * BLOCK
            for pr in range(2):
                qp = q_ref[0, pr, lo:lo + BLOCK, :]
                zero = jnp.zeros_like(qp)
                parts += [jnp.where(low, qp, zero), jnp.where(low, zero, qp)]
        return jnp.concatenate(parts, axis=0)

    col_max = {}

    def qk_chunk(u, c):
        lo = window_start(u) + c * kc
        t = _SWA_SHIFTS.index(window_start(u) - first_query(u))
        st = _dot_nt(k_ref[0, 0, lo:lo + kc, :], stacked_q(u)) + bias_ref[t, c * kc:(c + 1) * kc, :]
        col_max[u] = jnp.maximum(col_max.get(u, sink_row), jnp.max(st, axis=0, keepdims=True))
        return st

    def pv_chunk(u, c, scores):
        lo = window_start(u) + c * kc
        return _dot(vt_ref[0, 0, :, lo:lo + kc], jnp.exp2(scores[c] - col_max[u]).astype(BF16))

    def finish(u, acc):
        m = col_max.pop(u)
        ot = acc[:SWA_HD] / (acc[SWA_HD:SWA_HD + 1] + jnp.exp2(sink_row - m))
        for blk in range(SWA_UNIT_BLOCKS):
            lo = first_query(u) + blk * BLOCK
            for pr in range(2):
                c0 = (blk * SWA_GROUP + 2 * pr) * BLOCK
                pair = jnp.concatenate([ot[:, c0:c0 + BLOCK], ot[:, c0 + BLOCK:c0 + 2 * BLOCK]], axis=0)
                o_ref[0, lo:lo + BLOCK, pr * LANES:(pr + 1) * LANES] = pair.T.astype(o_ref.dtype)

    _pipelined_units(n_units, n_chunks, qk_chunk, pv_chunk, finish)


def _swa_attention(slopes, sink, q, k, v):
    B, _, S, _ = q.shape
    smem = pl.BlockSpec(memory_space=pltpu.SMEM)
    return pl.pallas_call(
        _swa_attn_kernel,
        grid=(SWA_KV_HEADS, B),
        in_specs=[
            smem, smem,
            pl.BlockSpec((1, 2, S, LANES), lambda g, b: (b, g, 0, 0)),
            pl.BlockSpec((1, 1, S, LANES), lambda g, b: (b, g, 0, 0)),
            pl.BlockSpec((1, 1, SWA_HD + ONES_ROWS, S), lambda g, b: (b, g, 0, 0)),
        ],
        out_specs=pl.BlockSpec((1, S, 2 * LANES), lambda g, b: (b, 0, g)),
        out_shape=jax.ShapeDtypeStruct((B, S, SWA_HEADS * SWA_HD), BF16),
        scratch_shapes=[pltpu.VMEM((len(_SWA_SHIFTS), SWA_UNIT_SPAN,
                                    SWA_UNIT_BLOCKS * SWA_GROUP * BLOCK), F32)],
        compiler_params=_params("arbitrary", "arbitrary"),
        name="swa_attention",
    )(slopes, sink, q, k, v)


def _post_kernel(oa_ref, ob_ref, x_ref, woa_ref, wob_ref, g1_ref, b1_ref,
                 w1_ref, fb1_ref, w2_ref, fb2_ref, g2_ref, b2_ref, out_ref):
    hm = x_ref.shape[0] // POST_SPLIT
    rows = [slice(h * hm, (h + 1) * hm) for h in range(POST_SPLIT)]
    y = [_dot(oa_ref[r, :], woa_ref[...]) + _dot(ob_ref[r, :], wob_ref[...]) for r in rows]
    x1 = [_layer_norm(ALPHA * x_ref[r, :] + y[h], g1_ref[...], b1_ref[...]) for h, r in enumerate(rows)]
    x1b = [v.astype(BF16) for v in x1]
    acc = [jnp.zeros_like(v) + fb2_ref[...] for v in x1]
    for c in range(0, w1_ref.shape[1], FF_CHUNK):
        for h in range(POST_SPLIT):
            hc = _dot(x1b[h], w1_ref[:, c:c + FF_CHUNK]) + fb1_ref[:, c:c + FF_CHUNK]
            hc = jnp.square(jnp.maximum(hc, 0.0))
            acc[h] = acc[h] + _dot(hc.astype(BF16), w2_ref[c:c + FF_CHUNK, :])
    for h, r in enumerate(rows):
        out_ref[r, :] = _layer_norm(ALPHA * x1[h] + acc[h], g2_ref[...], b2_ref[...])


def _post(o_a, o_b, a_blk, b_blk, x, wo_a, wo_b, g1, b1, w1, fb1, w2, fb2, g2, b2):
    T, D = x.shape
    tm = min(POST_SPLIT * POST_TILE, T)
    half = wo_a.shape[0]
    const = lambda i: (0, 0)
    resident = lambda a: pl.BlockSpec(a.shape, const, pipeline_mode=pl.Buffered(1))
    return pl.pallas_call(
        _post_kernel,
        grid=(T // tm,),
        in_specs=[
            pl.BlockSpec((tm, half), lambda i: (i, a_blk)),
            pl.BlockSpec((tm, half), lambda i: (i, b_blk)),
            pl.BlockSpec((tm, D), lambda i: (i, 0)),
            resident(wo_a), resident(wo_b), resident(g1), resident(b1),
            resident(w1), resident(fb1), resident(w2), resident(fb2), resident(g2), resident(b2),
        ],
        out_specs=pl.BlockSpec((tm, D), lambda i: (i, 0)),
        out_shape=jax.ShapeDtypeStruct((T, D), F32),
        compiler_params=_params("parallel"),
        name="post_mlp",
    )(o_a, o_b, x, wo_a, wo_b, g1, b1, w1, fb1, w2, fb2, g2, b2)


def _prep_even_weights(w_in, w_uq, w_ukv):
    D = w_in.shape[0]
    o_kr = MLA_Q_LORA + MLA_KV_LORA
    o_dq = o_kr + MLA_ROPE
    kr_slab = jnp.zeros((D, LANES), F32).at[:, MLA_NOPE:MLA_NOPE + MLA_ROPE].set(w_in[:, o_kr:o_dq])
    w_in_p = jnp.concatenate([w_in[:, :o_kr], kr_slab, w_in[:, o_dq:]], axis=1).astype(BF16)
    assert w_in_p.shape[1] == _EV_END

    r = w_uq.shape[0]
    qh = w_uq.reshape(r, MLA_HEADS, MLA_NOPE + MLA_ROPE)
    qh = jnp.pad(qh, ((0, 0), (0, 0), (0, LANES - MLA_NOPE - MLA_ROPE)))
    w_uq_p = qh.reshape(r, MLA_HEADS * LANES).astype(BF16)

    assert MLA_NOPE + MLA_V == LANES
    return w_in_p, w_uq_p, w_ukv.astype(BF16)


def _rope_tables(S):
    half = MLA_ROPE // 2
    inv = ROPE_THETA ** (-np.arange(half, dtype=np.float32) / half)
    ang = np.arange(S, dtype=np.float32)[:, None] * inv[None, :]
    cos, sin = np.cos(ang).astype(np.float32), np.sin(ang).astype(np.float32)
    c = np.zeros((S, LANES), np.float32)
    sn = np.zeros((S, LANES), np.float32)
    sp = np.zeros((S, LANES), np.float32)
    c[:, :MLA_NOPE] = 1.0
    c[:, MLA_NOPE:MLA_NOPE + half] = cos
    c[:, MLA_NOPE + half:MLA_NOPE + 2 * half] = cos
    sn[:, MLA_NOPE:MLA_NOPE + half] = -sin
    sp[:, MLA_NOPE + half:MLA_NOPE + 2 * half] = sin
    return jnp.asarray(c), jnp.asarray(sn), jnp.asarray(sp)


def _alibi_slopes_log2(n):
    return jnp.asarray(2.0 ** (-8.0 * np.arange(1, n + 1, dtype=np.float32) / n) * LOG2E, F32)


def kernel(x, ev_w_in, ev_q_norm, ev_kv_norm, ev_w_uq, ev_w_ukv, ev_lam_q1, ev_lam_k1, ev_lam_q2, ev_lam_k2, ev_diff_norm, ev_w_out, od_w_in, od_sink, od_w_out, ln1_g, ln1_b, ln2_g, ln2_b, ffn_w1, ffn_b1, ffn_w2, ffn_b2):
    B, S, D = x.shape
    row = lambda a: a.reshape(1, -1).astype(F32)
    rope_c, rope_sn, rope_sp = _rope_tables(S)
    xs = x
    for layer in range(DEPTH):
        j = layer // 2
        if layer % 2 == 0:
            lambda_init = 0.8 - 0.6 * math.exp(-0.3 * layer)
            w_in_p, w_uq_p, w_ukv_p = _prep_even_weights(ev_w_in[j], ev_w_uq[j], ev_w_ukv[j])
            q, k, v, dq, dk, dv = _even_in_proj(xs, w_in_p, row(ev_q_norm[j]), row(ev_kv_norm[j]),
                                                w_uq_p, w_ukv_p, rope_c, rope_sn, rope_sp)
            o_mla = _mla_attention(q, k, v)
            lam_vecs = jnp.stack([ev_lam_q1[j], ev_lam_k1[j], ev_lam_q2[j], ev_lam_k2[j]]).astype(F32)
            o_diff = _diff_attention(_alibi_slopes_log2(DIFF_HEADS), lam_vecs, dq, dk, dv,
                                     row(ev_diff_norm[j]), lambda_init)
            n_a = o_mla.shape[-1]
            o_a, o_b, a_blk, b_blk = o_mla.reshape(B * S, n_a), o_diff.reshape(B * S, -1), 0, 0
            w_out = ev_w_out[j]
        else:
            q, k, v = _odd_in_proj(xs, od_w_in[j].astype(BF16))
            o = _swa_attention(_alibi_slopes_log2(SWA_HEADS), od_sink[j].astype(F32), q, k, v)
            n_a = o.shape[-1] // 2
            o_a = o_b = o.reshape(B * S, -1)
            a_blk, b_blk = 0, 1
            w_out = od_w_out[j]
        xs = _post(o_a, o_b, a_blk, b_blk, xs.reshape(B * S, D),
                   w_out[:n_a].astype(BF16), w_out[n_a:].astype(BF16),
                   row(ln1_g[layer]), row(ln1_b[layer]),
                   ffn_w1[layer].astype(BF16), row(ffn_b1[layer]),
                   ffn_w2[layer].astype(BF16), row(ffn_b2[layer]),
                   row(ln2_g[layer]), row(ln2_b[layer])).reshape(B, S, D)
    return xs
```

```python
import functools
import math

import jax
import jax.numpy as jnp
import numpy as np
from jax import lax
from jax.experimental import pallas as pl
from jax.experimental.pallas import tpu as pltpu

F32 = jnp.float32
BF16 = jnp.bfloat16

D_MODEL = 1024
DEPTH = 2
MLA_HEADS = 8
MLA_NOPE = 64
MLA_ROPE = 32
MLA_V = 64
MLA_Q_LORA = 384
MLA_KV_LORA = 256
ROPE_THETA = 10000.0
DIFF_HEADS = 4
DIFF_HD = 64
DIFF_VD = 2 * DIFF_HD
SWA_HEADS = 16
SWA_KV_HEADS = 4
SWA_HD = 64
SWA_GROUP = SWA_HEADS // SWA_KV_HEADS
WINDOW = 128
BLOCK = 128
D_FF = 4 * D_MODEL
ALPHA = (2 * DEPTH) ** 0.25
LN_EPS = 1e-5
RMS_EPS = 1e-6
LOG2E = math.log2(math.e)

LANES = 128
HALF = LANES // 2

_EV_CQ = 0
_EV_CKV = _EV_CQ + MLA_Q_LORA
_EV_KR = _EV_CKV + MLA_KV_LORA
_EV_DQ = _EV_KR + LANES
_EV_DK = _EV_DQ + DIFF_HEADS * LANES
_EV_DV = _EV_DK + DIFF_HEADS * LANES
_EV_END = _EV_DV + DIFF_HEADS * LANES

VMEM_LIMIT_BYTES = 56 * 1024 * 1024

TOKEN_TILE = 512
POST_TILE = 256
FULL_QUERY_TILE = 2048
ATTN_UNIT = 512
ATTN_KEY_CHUNK = 256
ONES_ROWS = 16
FF_CHUNK = 1024
POST_SPLIT = 4
IN_SPLIT = 2

BOUND_SLACK = 1.0 + 2.0 ** -6
MIN_ROW_SUM = 2.0 ** -100


def _params(*sem):
    return pltpu.CompilerParams(dimension_semantics=sem, vmem_limit_bytes=VMEM_LIMIT_BYTES)


def _dot(a, b):
    return jnp.dot(a, b, preferred_element_type=F32)


def _dot_nt(a, b):
    return lax.dot_general(a, b, (((1,), (1,)), ((), ())), preferred_element_type=F32)


def _rms(x, g):
    return x * lax.rsqrt(jnp.mean(x * x, axis=-1, keepdims=True) + RMS_EPS) * g


def _layer_norm(x, g, b):
    mu = jnp.mean(x, axis=-1, keepdims=True)
    xc = x - mu
    var = jnp.mean(xc * xc, axis=-1, keepdims=True)
    return xc * lax.rsqrt(var + LN_EPS) * g + b


def _rope_slab(blk, c, s_next, s_prev):
    nxt = pltpu.roll(blk, LANES - MLA_ROPE // 2, 1)
    prv = pltpu.roll(blk, MLA_ROPE // 2, 1)
    return blk * c + nxt * s_next + prv * s_prev


def _sq_norm(x):
    xf = x.astype(F32)
    return jnp.sum(xf * xf, axis=1, keepdims=True)


def _max_sq_norm_tile(x):
    return jnp.broadcast_to(jnp.max(_sq_norm(x), axis=0, keepdims=True), (8, LANES))


def _row_sq_norms(x):
    ones = jnp.ones((8, x.shape[1]), BF16)
    return _dot_nt(ones, x * x)[0:1]


def _pipelined_units(n_units, n_chunks, qk_chunk, pv_chunk, finish=None):
    results = []
    cur = None
    for stage in range(n_units + 1):
        nxt = [] if stage < n_units else None
        acc = None
        for c in range(n_chunks):
            if nxt is not None:
                nxt.append(qk_chunk(stage, c))
            if cur is not None:
                d = pv_chunk(stage - 1, c, cur)
                acc = d if acc is None else acc + d
        if cur is not None:
            if finish is None:
                results.append(acc)
            else:
                finish(stage - 1, acc)
        cur = nxt
    return results


def _staged(n, front, back):
    carried = None
    for i in range(n + 1):
        nxt = front(i) if i < n else None
        if carried is not None:
            back(i - 1, carried)
        carried = nxt


def _all_rows_healthy(row_sums):
    return jnp.min(jnp.where(row_sums >= MIN_ROW_SUM, 1.0, 0.0)) > 0.5


def _even_in_kernel(x_ref, w_in_ref, qn_ref, kvn_ref, w_uq_ref, w_ukv_ref, c_ref, sn_ref, sp_ref,
                    q_ref, k_ref, v_ref, dq_ref, dk_ref, dv_ref):
    hm = x_ref.shape[1] // IN_SPLIT
    low = lax.broadcasted_iota(jnp.int32, (hm, LANES), 1) < HALF
    ones = jnp.ones((ONES_ROWS, hm), BF16)
    q_scale = (MLA_NOPE + MLA_ROPE) ** -0.5 * LOG2E
    dq_scale = DIFF_HD ** -0.5 * LOG2E

    def front(i):
        h = _dot(x_ref[0, i * hm:(i + 1) * hm, :].astype(BF16), w_in_ref[...])
        cq = _rms(h[:, _EV_CQ:_EV_CKV], qn_ref[...])
        q = _dot(cq.astype(BF16), w_uq_ref[...])
        ckv = _rms(h[:, _EV_CKV:_EV_KR], kvn_ref[...])
        kv = _dot(ckv.astype(BF16), w_ukv_ref[...])
        return h, q, kv

    def back(i, vals):
        h, q, kv = vals
        r = slice(i * hm, (i + 1) * hm)
        c, sn, sp = c_ref[r, :], sn_ref[r, :], sp_ref[r, :]
        for hd in range(MLA_HEADS):
            blk = q[:, hd * LANES:(hd + 1) * LANES]
            q_ref[0, hd, :, r] = (_rope_slab(blk, c, sn, sp) * q_scale).T.astype(BF16)
        kr = _rope_slab(h[:, _EV_KR:_EV_DQ], c, sn, sp)
        for hd in range(MLA_HEADS):
            slab = kv[:, hd * LANES:(hd + 1) * LANES]
            k_ref[0, hd, r, :] = (jnp.where(low, slab, 0.0) + kr).astype(BF16)
            v_ref[0, hd, :, r] = jnp.where(low, 1.0, slab).T.astype(BF16)
        for hd in range(DIFF_HEADS):
            dq_ref[0, hd, r, :] = (h[:, _EV_DQ + hd * LANES:_EV_DQ + (hd + 1) * LANES] * dq_scale).astype(BF16)
            dk_ref[0, hd, r, :] = h[:, _EV_DK + hd * LANES:_EV_DK + (hd + 1) * LANES].astype(BF16)
            dv_ref[0, hd, :LANES, r] = h[:, _EV_DV + hd * LANES:_EV_DV + (hd + 1) * LANES].T.astype(BF16)
            dv_ref[0, hd, LANES:, r] = ones

    _staged(IN_SPLIT, front, back)


def _even_in_proj(x, w_in_p, q_norm, kv_norm, w_uq_p, w_ukv_p, rope_c, rope_sn, rope_sp):
    B, S, D = x.shape
    tm = min(IN_SPLIT * TOKEN_TILE, S)
    nt = S // tm
    const = lambda b, i: (0, 0)
    head_out = lambda n, w=LANES: pl.BlockSpec((1, n, tm, w), lambda b, i: (b, 0, i, 0))
    head_shape = lambda n, w=LANES: jax.ShapeDtypeStruct((B, n, S, w), BF16)
    head_out_t = lambda n, r: pl.BlockSpec((1, n, r, tm), lambda b, i: (b, 0, 0, i))
    head_shape_t = lambda n, r: jax.ShapeDtypeStruct((B, n, r, S), BF16)
    return pl.pallas_call(
        _even_in_kernel,
        grid=(B, nt),
        in_specs=[
            pl.BlockSpec((1, tm, D), lambda b, i: (b, i, 0)),
            pl.BlockSpec(w_in_p.shape, const),
            pl.BlockSpec(q_norm.shape, const),
            pl.BlockSpec(kv_norm.shape, const),
            pl.BlockSpec(w_uq_p.shape, const),
            pl.BlockSpec(w_ukv_p.shape, const),
            pl.BlockSpec((tm, LANES), lambda b, i: (i, 0)),
            pl.BlockSpec((tm, LANES), lambda b, i: (i, 0)),
            pl.BlockSpec((tm, LANES), lambda b, i: (i, 0)),
        ],
        out_specs=[head_out_t(MLA_HEADS, LANES), head_out(MLA_HEADS), head_out_t(MLA_HEADS, LANES),
                   head_out(DIFF_HEADS), head_out(DIFF_HEADS), head_out_t(DIFF_HEADS, LANES + ONES_ROWS)],
        out_shape=[head_shape_t(MLA_HEADS, LANES), head_shape(MLA_HEADS), head_shape_t(MLA_HEADS, LANES),
                   head_shape(DIFF_HEADS), head_shape(DIFF_HEADS),
                   head_shape_t(DIFF_HEADS, LANES + ONES_ROWS)],
        compiler_params=_params("parallel", "parallel"),
        name="even_in_proj",
    )(x, w_in_p, q_norm, kv_norm, w_uq_p, w_ukv_p, rope_c, rope_sn, rope_sp)


def _mla_attn_kernel(q_ref, k_ref, vt_ref, o_ref, kmax_ref):
    S, tq = k_ref.shape[2], q_ref.shape[3]
    unit, kc = min(ATTN_UNIT, tq), min(ATTN_KEY_CHUNK, S)

    @pl.when(pl.program_id(2) == 0)
    def _():
        for hh in range(2):
            kmax_ref[hh] = _max_sq_norm_tile(k_ref[0, hh])

    units = [(hh, u) for u in range(tq // unit) for hh in range(2)]

    def q_of(i):
        hh, u = units[i]
        return q_ref[0, hh, :, u * unit:(u + 1) * unit]

    def sq_norms(qt):
        qf = qt.astype(F32)
        return jnp.sum(qf * qf, axis=0, keepdims=True)

    bounds = [jnp.sqrt(sq_norms(q_of(i)) * kmax_ref[units[i][0]][0:1, 0:1]) * BOUND_SLACK
              for i in range(len(units))]

    def qk_chunk(i, c):
        return _dot(k_ref[0, units[i][0], c * kc:(c + 1) * kc, :], q_of(i))

    def pv_chunk(i, c, scores):
        pt = jnp.exp2(scores[c] - bounds[i]).astype(BF16)
        return _dot(vt_ref[0, units[i][0], :, c * kc:(c + 1) * kc], pt)

    def write(accs):
        for u in range(tq // unit):
            ot = jnp.concatenate([a[HALF:] / a[:HALF] for a in accs[2 * u:2 * u + 2]], axis=0)
            o_ref[0, u * unit:(u + 1) * unit, :] = ot.T.astype(o_ref.dtype)

    accs = _pipelined_units(len(units), S // kc, qk_chunk, pv_chunk)
    write(accs)
    sums = accs[0][:8]
    for a in accs[1:]:
        sums = jnp.minimum(sums, a[:8])
    healthy = _all_rows_healthy(sums)

    @pl.when(jnp.logical_not(healthy))
    def _():
        exact = []
        for i, (hh, _) in enumerate(units):
            st = _dot(k_ref[0, hh], q_of(i))
            pt = jnp.exp2(st - jnp.max(st, axis=0, keepdims=True)).astype(BF16)
            exact.append(_dot(vt_ref[0, hh], pt))
        write(exact)


def _mla_attention(q, k, v):
    B, H, S, _ = k.shape
    tq = min(FULL_QUERY_TILE, S)
    return pl.pallas_call(
        _mla_attn_kernel,
        grid=(B, H // 2, S // tq),
        in_specs=[
            pl.BlockSpec((1, 2, LANES, tq), lambda b, j, i: (b, j, 0, i)),
            pl.BlockSpec((1, 2, S, LANES), lambda b, j, i: (b, j, 0, 0)),
            pl.BlockSpec((1, 2, LANES, S), lambda b, j, i: (b, j, 0, 0)),
        ],
        out_specs=pl.BlockSpec((1, tq, LANES), lambda b, j, i: (b, i, j)),
        out_shape=jax.ShapeDtypeStruct((B, S, H * MLA_V), BF16),
        scratch_shapes=[pltpu.VMEM((2, 8, LANES), F32)],
        compiler_params=_params("parallel", "parallel", "arbitrary"),
        name="mla_attention",
    )(q, k, v)


def _diff_attn_kernel(slope_ref, lam_ref, q_ref, k_ref, vt_ref, g_ref, o_ref, kmax_ref, *, lambda_init):
    hd = pl.program_id(1)
    qi = pl.program_id(2)
    S, tq = k_ref.shape[2], q_ref.shape[2]
    unit, kc = min(ATTN_UNIT, tq), min(ATTN_KEY_CHUNK, S)
    slope = slope_ref[hd]

    @pl.when(qi == 0)
    def _():
        k = k_ref[0, 0]
        k_lane = lax.broadcasted_iota(jnp.int32, k.shape, 1)
        k_zero = jnp.zeros_like(k)
        kmax_ref[0] = _max_sq_norm_tile(jnp.where(k_lane < HALF, k, k_zero))
        kmax_ref[1] = _max_sq_norm_tile(jnp.where(k_lane >= HALF, k, k_zero))

    lv = lam_ref[...]
    lam = (jnp.exp(jnp.sum(lv[0:1] * lv[1:2], axis=-1, keepdims=True))
           - jnp.exp(jnp.sum(lv[2:3] * lv[3:4], axis=-1, keepdims=True)) + lambda_init)

    units = [(mp, u) for u in range(tq // unit) for mp in range(2)]
    lane = lax.broadcasted_iota(jnp.int32, (unit, LANES), 1)

    def q_of(i):
        mp, u = units[i]
        q = q_ref[0, 0, u * unit:(u + 1) * unit, :]
        return jnp.where((lane < HALF) if mp == 0 else (lane >= HALF), q, jnp.zeros_like(q))

    def q_pos(i):
        u = units[i][1]
        return (qi * tq + u * unit + lax.broadcasted_iota(jnp.int32, (1, unit), 1)).astype(F32) * slope

    def k_pos(c, n):
        return (c + lax.broadcasted_iota(jnp.int32, (n, 1), 0)).astype(F32) * slope

    bounds = [jnp.sqrt(_row_sq_norms(q_of(i)) * kmax_ref[units[i][0]][0:1, 0:1]) * BOUND_SLACK
              for i in range(len(units))]

    def qk_chunk(i, c):
        return _dot_nt(k_ref[0, 0, c * kc:(c + 1) * kc, :], q_of(i))

    def pv_chunk(i, c, scores):
        dist = jnp.abs(k_pos(c * kc, kc) - q_pos(i))
        pt = jnp.exp2(scores[c] - dist - bounds[i]).astype(BF16)
        return _dot(vt_ref[0, 0, :, c * kc:(c + 1) * kc], pt)

    def write(accs):
        for u in range(tq // unit):
            outs = [a[:LANES] / a[LANES:LANES + 1] for a in accs[2 * u:2 * u + 2]]
            o = (outs[0] - lam * outs[1]).T
            o = _rms(o, g_ref[...]) * (1.0 - lambda_init)
            o_ref[0, u * unit:(u + 1) * unit, :] = o.astype(o_ref.dtype)

    accs = _pipelined_units(len(units), S // kc, qk_chunk, pv_chunk)
    write(accs)
    sums = accs[0][LANES:LANES + 8]
    for a in accs[1:]:
        sums = jnp.minimum(sums, a[LANES:LANES + 8])
    healthy = _all_rows_healthy(sums)

    @pl.when(jnp.logical_not(healthy))
    def _():
        exact = []
        for i in range(len(units)):
            st = _dot_nt(k_ref[0, 0], q_of(i)) - jnp.abs(k_pos(0, S) - q_pos(i))
            pt = jnp.exp2(st - jnp.max(st, axis=0, keepdims=True)).astype(BF16)
            exact.append(_dot(vt_ref[0, 0], pt))
        write(exact)


def _diff_attention(slopes, lam_vecs, q, k, v, diff_norm, lambda_init):
    B, H, S, _ = q.shape
    tq = min(FULL_QUERY_TILE, S)
    smem = pl.BlockSpec(memory_space=pltpu.SMEM)
    return pl.pallas_call(
        functools.partial(_diff_attn_kernel, lambda_init=lambda_init),
        grid=(B, H, S // tq),
        in_specs=[
            smem,
            pl.BlockSpec(lam_vecs.shape, lambda b, h, i: (0, 0)),
            pl.BlockSpec((1, 1, tq, LANES), lambda b, h, i: (b, h, i, 0)),
            pl.BlockSpec((1, 1, S, LANES), lambda b, h, i: (b, h, 0, 0)),
            pl.BlockSpec((1, 1, LANES + ONES_ROWS, S), lambda b, h, i: (b, h, 0, 0)),
            pl.BlockSpec(diff_norm.shape, lambda b, h, i: (0, 0)),
        ],
        out_specs=pl.BlockSpec((1, tq, LANES), lambda b, h, i: (b, i, h)),
        out_shape=jax.ShapeDtypeStruct((B, S, H * DIFF_VD), BF16),
        scratch_shapes=[pltpu.VMEM((2, 8, LANES), F32)],
        compiler_params=_params("parallel", "parallel", "arbitrary"),
        name="diff_attention",
    )(slopes, lam_vecs, q, k, v, diff_norm)


def _odd_in_kernel(x_ref, w_ref, q_ref, k_ref, v_ref):
    hm = x_ref.shape[1] // IN_SPLIT
    nq = SWA_HEADS // 2
    nkv = SWA_KV_HEADS // 2
    q_scale = SWA_HD ** -0.5 * LOG2E
    low = lax.broadcasted_iota(jnp.int32, (hm, LANES), 1) < HALF
    ones = jnp.ones((ONES_ROWS, hm), BF16)

    def front(i):
        return _dot(x_ref[0, i * hm:(i + 1) * hm, :].astype(BF16), w_ref[...])

    def back(i, h):
        r = slice(i * hm, (i + 1) * hm)
        for pr in range(nq):
            q_ref[0, pr, r, :] = (h[:, pr * LANES:(pr + 1) * LANES] * q_scale).astype(BF16)
        for j in range(nkv):
            kp = h[:, (nq + j) * LANES:(nq + j + 1) * LANES]
            swapped = pltpu.roll(kp, HALF, 1)
            k_ref[0, 2 * j, r, :] = jnp.where(low, kp, swapped).astype(BF16)
            k_ref[0, 2 * j + 1, r, :] = jnp.where(low, swapped, kp).astype(BF16)
            vt = h[:, (nq + nkv + j) * LANES:(nq + nkv + j + 1) * LANES].T.astype(BF16)
            for half in range(2):
                v_ref[0, 2 * j + half, :SWA_HD, r] = vt[half * SWA_HD:(half + 1) * SWA_HD]
                v_ref[0, 2 * j + half, SWA_HD:, r] = ones

    _staged(IN_SPLIT, front, back)


def _odd_in_proj(x, w_p):
    B, S, D = x.shape
    tm = min(IN_SPLIT * TOKEN_TILE, S)
    head_out = lambda n, w=LANES: pl.BlockSpec((1, n, tm, w), lambda b, i: (b, 0, i, 0))
    head_shape = lambda n, w=LANES: jax.ShapeDtypeStruct((B, n, S, w), BF16)
    return pl.pallas_call(
        _odd_in_kernel,
        grid=(B, S // tm),
        in_specs=[
            pl.BlockSpec((1, tm, D), lambda b, i: (b, i, 0)),
            pl.BlockSpec(w_p.shape, lambda b, i: (0, 0)),
        ],
        out_specs=[head_out(SWA_HEADS // 2), head_out(SWA_KV_HEADS),
                   pl.BlockSpec((1, SWA_KV_HEADS, SWA_HD + ONES_ROWS, tm), lambda b, i: (b, 0, 0, i))],
        out_shape=[head_shape(SWA_HEADS // 2), head_shape(SWA_KV_HEADS),
                   jax.ShapeDtypeStruct((B, SWA_KV_HEADS, SWA_HD + ONES_ROWS, S), BF16)],
        compiler_params=_params("parallel", "parallel"),
        name="odd_in_proj",
    )(x, w_p)


SWA_UNIT_BLOCKS = 2
SWA_UNIT_SPAN = SWA_UNIT_BLOCKS * BLOCK + 2 * WINDOW
SWA_KEY_CHUNK = 512
_SWA_SHIFTS = (-WINDOW, 0, -2 * WINDOW)


def _swa_attn_kernel(slope_ref, sink_ref, q_ref, k_ref, vt_ref, o_ref, bias_ref):
    g = pl.program_id(0)
    S = k_ref.shape[2]
    span = SWA_UNIT_SPAN
    n_units = S // (SWA_UNIT_BLOCKS * BLOCK)
    kc = SWA_KEY_CHUNK
    n_chunks = span // kc
    cols = SWA_UNIT_BLOCKS * SWA_GROUP * BLOCK
    low = lax.broadcasted_iota(jnp.int32, (BLOCK, LANES), 1) < HALF
    col = lax.broadcasted_iota(jnp.int32, (1, cols), 1)
    head_of_col = (col // BLOCK) % SWA_GROUP
    slope_row = jnp.zeros((1, cols), F32)
    sink_row = jnp.zeros((1, cols), F32)
    for hh in range(SWA_GROUP):
        slope_row = jnp.where(head_of_col == hh, slope_ref[g * SWA_GROUP + hh], slope_row)
        sink_row = jnp.where(head_of_col == hh, sink_ref[g * SWA_GROUP + hh] * LOG2E, sink_row)

    @pl.when(pl.program_id(1) == 0)
    def _():
        q_in_unit = col // (SWA_GROUP * BLOCK) * BLOCK + col % BLOCK
        rel0 = lax.broadcasted_iota(jnp.int32, (span, cols), 0) - q_in_unit
        for t, shift in enumerate(_SWA_SHIFTS):
            dist = jnp.abs(rel0 + shift)
            bias_ref[t] = jnp.where(dist <= WINDOW, -slope_row * dist.astype(F32), -jnp.inf)

    def first_query(u):
        return u * SWA_UNIT_BLOCKS * BLOCK

    def window_start(u):
        return min(max(first_query(u) - WINDOW, 0), S - span)

    def stacked_q(u):
        parts = []
        for blk in range(SWA_UNIT_BLOCKS):
            lo = first_query(u) + blk * BLOCK
            for pr in range(2):
                qp = q_ref[0, pr, lo:lo + BLOCK, :]
                zero = jnp.zeros_like(qp)
                parts += [jnp.where(low, qp, zero), jnp.where(low, zero, qp)]
        return jnp.concatenate(parts, axis=0)

    col_max = {}

    def qk_chunk(u, c):
        lo = window_start(u) + c * kc
        t = _SWA_SHIFTS.index(window_start(u) - first_query(u))
        st = _dot_nt(k_ref[0, 0, lo:lo + kc, :], stacked_q(u)) + bias_ref[t, c * kc:(c + 1) * kc, :]
        col_max[u] = jnp.maximum(col_max.get(u, sink_row), jnp.max(st, axis=0, keepdims=True))
        return st

    def pv_chunk(u, c, scores):
        lo = window_start(u) + c * kc
        return _dot(vt_ref[0, 0, :, lo:lo + kc], jnp.exp2(scores[c] - col_max[u]).astype(BF16))

    def finish(u, acc):
        m = col_max.pop(u)
        ot = acc[:SWA_HD] / (acc[SWA_HD:SWA_HD + 1] + jnp.exp2(sink_row - m))
        for blk in range(SWA_UNIT_BLOCKS):
            lo = first_query(u) + blk * BLOCK
            for pr in range(2):
                c0 = (blk * SWA_GROUP + 2 * pr) * BLOCK
                pair = jnp.concatenate([ot[:, c0:c0 + BLOCK], ot[:, c0 + BLOCK:c0 + 2 * BLOCK]], axis=0)
                o_ref[0, lo:lo + BLOCK, pr * LANES:(pr + 1) * LANES] = pair.T.astype(o_ref.dtype)

    _pipelined_units(n_units, n_chunks, qk_chunk, pv_chunk, finish)


def _swa_attention(slopes, sink, q, k, v):
    B, _, S, _ = q.shape
    smem = pl.BlockSpec(memory_space=pltpu.SMEM)
    return pl.pallas_call(
        _swa_attn_kernel,
        grid=(SWA_KV_HEADS, B),
        in_specs=[
            smem, smem,
            pl.BlockSpec((1, 2, S, LANES), lambda g, b: (b, g, 0, 0)),
            pl.BlockSpec((1, 1, S, LANES), lambda g, b: (b, g, 0, 0)),
            pl.BlockSpec((1, 1, SWA_HD + ONES_ROWS, S), lambda g, b: (b, g, 0, 0)),
        ],
        out_specs=pl.BlockSpec((1, S, 2 * LANES), lambda g, b: (b, 0, g)),
        out_shape=jax.ShapeDtypeStruct((B, S, SWA_HEADS * SWA_HD), BF16),
        scratch_shapes=[pltpu.VMEM((len(_SWA_SHIFTS), SWA_UNIT_SPAN,
                                    SWA_UNIT_BLOCKS * SWA_GROUP * BLOCK), F32)],
        compiler_params=_params("arbitrary", "arbitrary"),
        name="swa_attention",
    )(slopes, sink, q, k, v)


def _post_kernel(oa_ref, ob_ref, x_ref, woa_ref, wob_ref, g1_ref, b1_ref,
                 w1_ref, fb1_ref, w2_ref, fb2_ref, g2_ref, b2_ref, out_ref):
    hm = x_ref.shape[0] // POST_SPLIT
    rows = [slice(h * hm, (h + 1) * hm) for h in range(POST_SPLIT)]
    y = [_dot(oa_ref[r, :], woa_ref[...]) + _dot(ob_ref[r, :], wob_ref[...]) for r in rows]
    x1 = [_layer_norm(ALPHA * x_ref[r, :] + y[h], g1_ref[...], b1_ref[...]) for h, r in enumerate(rows)]
    x1b = [v.astype(BF16) for v in x1]
    acc = [jnp.zeros_like(v) + fb2_ref[...] for v in x1]
    for c in range(0, w1_ref.shape[1], FF_CHUNK):
        for h in range(POST_SPLIT):
            hc = _dot(x1b[h], w1_ref[:, c:c + FF_CHUNK]) + fb1_ref[:, c:c + FF_CHUNK]
            hc = jnp.square(jnp.maximum(hc, 0.0))
            acc[h] = acc[h] + _dot(hc.astype(BF16), w2_ref[c:c + FF_CHUNK, :])
    for h, r in enumerate(rows):
        out_ref[r, :] = _layer_norm(ALPHA * x1[h] + acc[h], g2_ref[...], b2_ref[...])


def _post(o_a, o_b, a_blk, b_blk, x, wo_a, wo_b, g1, b1, w1, fb1, w2, fb2, g2, b2):
    T, D = x.shape
    tm = min(POST_SPLIT * POST_TILE, T)
    half = wo_a.shape[0]
    const = lambda i: (0, 0)
    resident = lambda a: pl.BlockSpec(a.shape, const, pipeline_mode=pl.Buffered(1))
    return pl.pallas_call(
        _post_kernel,
        grid=(T // tm,),
        in_specs=[
            pl.BlockSpec((tm, half), lambda i: (i, a_blk)),
            pl.BlockSpec((tm, half), lambda i: (i, b_blk)),
            pl.BlockSpec((tm, D), lambda i: (i, 0)),
            resident(wo_a), resident(wo_b), resident(g1), resident(b1),
            resident(w1), resident(fb1), resident(w2), resident(fb2), resident(g2), resident(b2),
        ],
        out_specs=pl.BlockSpec((tm, D), lambda i: (i, 0)),
        out_shape=jax.ShapeDtypeStruct((T, D), F32),
        compiler_params=_params("parallel"),
        name="post_mlp",
    )(o_a, o_b, x, wo_a, wo_b, g1, b1, w1, fb1, w2, fb2, g2, b2)


def _prep_even_weights(w_in, w_uq, w_ukv):
    D = w_in.shape[0]
    o_kr = MLA_Q_LORA + MLA_KV_LORA
    o_dq = o_kr + MLA_ROPE
    kr_slab = jnp.zeros((D, LANES), F32).at[:, MLA_NOPE:MLA_NOPE + MLA_ROPE].set(w_in[:, o_kr:o_dq])
    w_in_p = jnp.concatenate([w_in[:, :o_kr], kr_slab, w_in[:, o_dq:]], axis=1).astype(BF16)
    assert w_in_p.shape[1] == _EV_END

    r = w_uq.shape[0]
    qh = w_uq.reshape(r, MLA_HEADS, MLA_NOPE + MLA_ROPE)
    qh = jnp.pad(qh, ((0, 0), (0, 0), (0, LANES - MLA_NOPE - MLA_ROPE)))
    w_uq_p = qh.reshape(r, MLA_HEADS * LANES).astype(BF16)

    assert MLA_NOPE + MLA_V == LANES
    return w_in_p, w_uq_p, w_ukv.astype(BF16)


def _rope_tables(S):
    half = MLA_ROPE // 2
    inv = ROPE_THETA ** (-np.arange(half, dtype=np.float32) / half)
    ang = np.arange(S, dtype=np.float32)[:, None] * inv[None, :]
    cos, sin = np.cos(ang).astype(np.float32), np.sin(ang).astype(np.float32)
    c = np.zeros((S, LANES), np.float32)
    sn = np.zeros((S, LANES), np.float32)
    sp = np.zeros((S, LANES), np.float32)
    c[:, :MLA_NOPE] = 1.0
    c[:, MLA_NOPE:MLA_NOPE + half] = cos
    c[:, MLA_NOPE + half:MLA_NOPE + 2 * half] = cos
    sn[:, MLA_NOPE:MLA_NOPE + half] = -sin
    sp[:, MLA_NOPE + half:MLA_NOPE + 2 * half] = sin
    return jnp.asarray(c), jnp.asarray(sn), jnp.asarray(sp)


def _alibi_slopes_log2(n):
    return jnp.asarray(2.0 ** (-8.0 * np.arange(1, n + 1, dtype=np.float32) / n) * LOG2E, F32)


def kernel(x, ev_w_in, ev_q_norm, ev_kv_norm, ev_w_uq, ev_w_ukv, ev_lam_q1, ev_lam_k1, ev_lam_q2, ev_lam_k2, ev_diff_norm, ev_w_out, od_w_in, od_sink, od_w_out, ln1_g, ln1_b, ln2_g, ln2_b, ffn_w1, ffn_b1, ffn_w2, ffn_b2):
    B, S, D = x.shape
    row = lambda a: a.reshape(1, -1).astype(F32)
    rope_c, rope_sn, rope_sp = _rope_tables(S)
    xs = x
    for layer in range(DEPTH):
        j = layer // 2
        if layer % 2 == 0:
            lambda_init = 0.8 - 0.6 * math.exp(-0.3 * layer)
            w_in_p, w_uq_p, w_ukv_p = _prep_even_weights(ev_w_in[j], ev_w_uq[j], ev_w_ukv[j])
            q, k, v, dq, dk, dv = _even_in_proj(xs, w_in_p, row(ev_q_norm[j]), row(ev_kv_norm[j]),
                                                w_uq_p, w_ukv_p, rope_c, rope_sn, rope_sp)
            o_mla = _mla_attention(q, k, v)
            lam_vecs = jnp.stack([ev_lam_q1[j], ev_lam_k1[j], ev_lam_q2[j], ev_lam_k2[j]]).astype(F32)
            o_diff = _diff_attention(_alibi_slopes_log2(DIFF_HEADS), lam_vecs, dq, dk, dv,
                                     row(ev_diff_norm[j]), lambda_init)
            n_a = o_mla.shape[-1]
            o_a, o_b, a_blk, b_blk = o_mla.reshape(B * S, n_a), o_diff.reshape(B * S, -1), 0, 0
            w_out = ev_w_out[j]
        else:
            q, k, v = _odd_in_proj(xs, od_w_in[j].astype(BF16))
            o = _swa_attention(_alibi_slopes_log2(SWA_HEADS), od_sink[j].astype(F32), q, k, v)
            n_a = o.shape[-1] // 2
            o_a = o_b = o.reshape(B * S, -1)
            a_blk, b_blk = 0, 1
            w_out = od_w_out[j]
        xs = _post(o_a, o_b, a_blk, b_blk, xs.reshape(B * S, D),
                   w_out[:n_a].astype(BF16), w_out[n_a:].astype(BF16),
                   row(ln1_g[layer]), row(ln1_b[layer]),
                   ffn_w1[layer].astype(BF16), row(ffn_b1[layer]),
                   ffn_w2[layer].astype(BF16), row(ffn_b2[layer]),
                   row(ln2_g[layer]), row(ln2_b[layer])).reshape(B, S, D)
    return xs
```
